```python
import math
import jax, jax.numpy as jnp
from jax import lax
import numpy as np

D_MODEL = 4096
BATCH = 4
SEQ = 2048
DEPTH = 4
DEC_BATCH = 1
DEC_SEQ = 16384
PAST_LEN = 128

N_MIXERS = 3
GRID_W = 64
Q_BLOCK = 128
EPS = 1e-6

A_HEADS = 32
A_KV_HEADS = 8
A_GROUP = A_HEADS // A_KV_HEADS
A_HEAD_DIM = D_MODEL // A_HEADS
A_QKV_OUT = D_MODEL + 2 * A_KV_HEADS * A_HEAD_DIM
ROPE_THETA = 10000.0
CONV_WIDTH = 31
C_HEAD_DIM = 128
C_HEADS = D_MODEL // (2 * C_HEAD_DIM)
D_FF = 2 * D_MODEL
N_EXPERTS = 8
TOP_K = 2
D_FF_EXPERT = D_MODEL // 8

N_A = (DEPTH + 2) // 3
N_B = (DEPTH + 1) // 3
N_C = DEPTH // 3
N_DENSE = (DEPTH + 1) // 2
N_MOE = DEPTH // 2

kernel_name = "hybrid_bidir_gqa_conformer_diffattn_moe"


def rmsnorm(x, g):
    xf = x.astype(jnp.float32)
    y = xf * lax.rsqrt(jnp.mean(xf * xf, axis=-1, keepdims=True) + EPS)
    return (y * g.astype(jnp.float32)).astype(x.dtype)


def layernorm(x, g, b):
    xf = x.astype(jnp.float32)
    mu = jnp.mean(xf, axis=-1, keepdims=True)
    var = jnp.mean(jnp.square(xf - mu), axis=-1, keepdims=True)
    y = (xf - mu) * lax.rsqrt(var + EPS)
    return (y * g.astype(jnp.float32) + b.astype(jnp.float32)).astype(x.dtype)


def rotate_half(x):
    x1, x2 = jnp.split(x, 2, axis=-1)
    return jnp.concatenate([-x2, x1], axis=-1)


def axial_rope_tables(S):
    rows = S // GRID_W
    row = jnp.repeat(jnp.arange(rows, dtype=jnp.float32), GRID_W)
    col = jnp.tile(jnp.arange(GRID_W, dtype=jnp.float32), rows)
    half = A_HEAD_DIM // 2
    inv_freq = ROPE_THETA ** (-jnp.arange(0, half, 2, dtype=jnp.float32) / half)
    ang_r = row[:, None] * inv_freq[None, :]
    ang_c = col[:, None] * inv_freq[None, :]
    emb = jnp.concatenate([ang_r, ang_r, ang_c, ang_c], axis=-1)
    return jnp.cos(emb), jnp.sin(emb)


def apply_axial_rope(x, cos, sin):
    half = A_HEAD_DIM // 2
    rot = jnp.concatenate([rotate_half(x[..., :half]), rotate_half(x[..., half:])], axis=-1)
    return (x * cos + rot * sin).astype(x.dtype)


def gqa_axial_attention(h, w_qkv, w_o, g_q, g_k):
    B, S, _ = h.shape
    nblk = S // Q_BLOCK
    qkv = h @ w_qkv
    kv_w = A_KV_HEADS * A_HEAD_DIM
    q = qkv[..., :D_MODEL].reshape(B, S, A_KV_HEADS, A_GROUP, A_HEAD_DIM)
    k = qkv[..., D_MODEL:D_MODEL + kv_w].reshape(B, S, A_KV_HEADS, A_HEAD_DIM)
    v = qkv[..., D_MODEL + kv_w:].reshape(B, S, A_KV_HEADS, A_HEAD_DIM)
    q = rmsnorm(q, g_q)
    k = rmsnorm(k, g_k)
    cos, sin = axial_rope_tables(S)
    q = apply_axial_rope(q, cos[:, None, None, :], sin[:, None, None, :])
    k = apply_axial_rope(k, cos[:, None, :], sin[:, None, :])
    scale = A_HEAD_DIM ** -0.5
    qb = q.reshape(B, nblk, Q_BLOCK, A_KV_HEADS, A_GROUP, A_HEAD_DIM).transpose(1, 0, 2, 3, 4, 5)

    def block(qi):
        s = jnp.einsum('bqkgd,bskd->bkgqs', qi, k, preferred_element_type=jnp.float32) * scale
        s_max = jnp.max(s, axis=-1, keepdims=True)
        p = jnp.exp(s - s_max)
        denom = jnp.sum(p, axis=-1)
        o = jnp.einsum('bkgqs,bskd->bqkgd', p.astype(v.dtype), v)
        return o / denom.transpose(0, 3, 1, 2)[..., None].astype(o.dtype)

    o = lax.map(block, qb)
    o = o.transpose(1, 0, 2, 3, 4, 5).reshape(B, S, D_MODEL)
    return o @ w_o


def conformer_conv(h, w_pw1, b_pw1, w_dw, b_dw, ln_g, ln_b, w_pw2, b_pw2):
    u = h @ w_pw1 + b_pw1
    a, gate = jnp.split(u, 2, axis=-1)
    u = a * jax.nn.sigmoid(gate)
    u = lax.conv_general_dilated(
        u, w_dw[:, None, :], window_strides=(1,),
        padding=[(CONV_WIDTH // 2, CONV_WIDTH // 2)],
        dimension_numbers=('NWC', 'WIO', 'NWC'),
        feature_group_count=D_MODEL) + b_dw
    u = jax.nn.silu(layernorm(u, ln_g, ln_b))
    return u @ w_pw2 + b_pw2


def diff_attention(h, w_qkv, w_o, g_q, g_k, lq1, lk1, lq2, lk2, g_sub, lambda_init):
    B, S, _ = h.shape
    nblk = S // Q_BLOCK
    qkv = h @ w_qkv
    q = qkv[..., :D_MODEL].reshape(B, S, C_HEADS, 2, C_HEAD_DIM)
    k = qkv[..., D_MODEL:2 * D_MODEL].reshape(B, S, C_HEADS, 2, C_HEAD_DIM)
    v = qkv[..., 2 * D_MODEL:].reshape(B, S, C_HEADS, 2 * C_HEAD_DIM)
    q = rmsnorm(q, g_q)
    k = rmsnorm(k, g_k)
    f32 = jnp.float32
    lam = (jnp.exp(jnp.sum(lq1.astype(f32) * lk1.astype(f32)))
           - jnp.exp(jnp.sum(lq2.astype(f32) * lk2.astype(f32))) + lambda_init)
    slopes = 2.0 ** (-8.0 * jnp.arange(1, C_HEADS + 1, dtype=f32) / C_HEADS)
    pos = jnp.arange(S, dtype=f32)
    scale = C_HEAD_DIM ** -0.5
    qb = q.reshape(B, nblk, Q_BLOCK, C_HEADS, 2, C_HEAD_DIM).transpose(1, 0, 2, 3, 4, 5)
    qpos = pos.reshape(nblk, Q_BLOCK)

    def block(args):
        qi, qp = args
        alibi = -slopes[:, None, None] * jnp.abs(qp[:, None] - pos[None, :])
        s = jnp.einsum('bqhcd,bshcd->bhcqs', qi, k, preferred_element_type=f32) * scale
        s = s + alibi[None, :, None]
        s_max = jnp.max(s, axis=-1, keepdims=True)
        p = jnp.exp(s - s_max)
        inv = 1.0 / jnp.sum(p, axis=-1, keepdims=True)
        pd = p[:, :, 0] * inv[:, :, 0] - p[:, :, 1] * (lam * inv[:, :, 1])
        return jnp.einsum('bhqs,bshe->bqhe', pd.astype(v.dtype), v)

    o = lax.map(block, (qb, qpos))
    o = o.transpose(1, 0, 2, 3, 4).reshape(B, S, C_HEADS, 2 * C_HEAD_DIM)
    o = rmsnorm(o, g_sub) * (1.0 - lambda_init)
    return o.reshape(B, S, D_MODEL) @ w_o


def swiglu(h, w1, w3, w2):
    return (jax.nn.silu(h @ w1) * (h @ w3)) @ w2


def moe_swiglu(h, w_router, w1, w3, w2):
    logits = (h @ w_router).astype(jnp.float32)
    top_vals, top_idx = lax.top_k(logits, TOP_K)
    gates = jax.nn.softmax(top_vals, axis=-1)
    dense_gates = jnp.sum(jax.nn.one_hot(top_idx, N_EXPERTS, dtype=jnp.float32)
                          * gates[..., None], axis=-2)
    a = jnp.einsum('bsd,edf->bsef', h, w1)
    b = jnp.einsum('bsd,edf->bsef', h, w3)
    u = jax.nn.silu(a) * b * dense_gates[..., None].astype(h.dtype)
    return jnp.einsum('bsef,efd->bsd', u, w2)


def trunk(x, params):
    for i in range(DEPTH):
        h = rmsnorm(x, params['norm_mix'][i])
        m, j = i % N_MIXERS, i // N_MIXERS
        if m == 0:
            out = gqa_axial_attention(h, params['a_w_qkv'][j], params['a_w_o'][j],
                                      params['a_q_norm'][j], params['a_k_norm'][j])
        elif m == 1:
            out = conformer_conv(h, params['b_w_pw1'][j], params['b_b_pw1'][j],
                                 params['b_w_dw'][j], params['b_b_dw'][j],
                                 params['b_ln_g'][j], params['b_ln_b'][j],
                                 params['b_w_pw2'][j], params['b_b_pw2'][j])
        else:
            lambda_init = 0.8 - 0.6 * math.exp(-0.3 * i)
            out = diff_attention(h, params['c_w_qkv'][j], params['c_w_o'][j],
                                 params['c_q_norm'][j], params['c_k_norm'][j],
                                 params['c_lambda_q1'][j], params['c_lambda_k1'][j],
                                 params['c_lambda_q2'][j], params['c_lambda_k2'][j],
                                 params['c_subln'][j], lambda_init)
        x = x + out
        h = rmsnorm(x, params['norm_ffn'][i])
        k = i // 2
        if i % 2 == 0:
            out = swiglu(h, params['ffn_w1'][k], params['ffn_w3'][k], params['ffn_w2'][k])
        else:
            out = moe_swiglu(h, params['moe_router'][k], params['moe_w1'][k],
                             params['moe_w3'][k], params['moe_w2'][k])
        x = x + out
    return x


def setup_inputs(seed: int = 0) -> dict:
    key = jax.random.key(seed)
    keys = iter(jax.random.split(key, 64))
    f32 = jnp.float32
    out_scale = (2.0 * DEPTH) ** -0.5

    def normal(shape, scale):
        return jax.random.normal(next(keys), shape, dtype=f32) * scale

    def gain(shape):
        return 1.0 + normal(shape, 0.02)

    D = D_MODEL
    return {
        "x_prompt": normal((BATCH, SEQ, D), 1.0),
        "x_sample": normal((DEC_BATCH, DEC_SEQ, D), 1.0),
        "norm_mix": gain((DEPTH, D)),
        "norm_ffn": gain((DEPTH, D)),
        "a_w_qkv": normal((N_A, D, A_QKV_OUT), D ** -0.5),
        "a_w_o": normal((N_A, D, D), D ** -0.5 * out_scale),
        "a_q_norm": gain((N_A, A_HEAD_DIM)),
        "a_k_norm": gain((N_A, A_HEAD_DIM)),
        "b_w_pw1": normal((N_B, D, 2 * D), D ** -0.5),
        "b_b_pw1": normal((N_B, 2 * D), 0.01),
        "b_w_dw": normal((N_B, CONV_WIDTH, D), CONV_WIDTH ** -0.5),
        "b_b_dw": normal((N_B, D), 0.01),
        "b_ln_g": gain((N_B, D)),
        "b_ln_b": normal((N_B, D), 0.01),
        "b_w_pw2": normal((N_B, D, D), D ** -0.5 * out_scale),
        "b_b_pw2": normal((N_B, D), 0.01),
        "c_w_qkv": normal((N_C, D, 3 * D), D ** -0.5),
        "c_w_o": normal((N_C, D, D), D ** -0.5 * out_scale),
        "c_q_norm": gain((N_C, C_HEAD_DIM)),
        "c_k_norm": gain((N_C, C_HEAD_DIM)),
        "c_lambda_q1": normal((N_C, C_HEAD_DIM), 0.1),
        "c_lambda_k1": normal((N_C, C_HEAD_DIM), 0.1),
        "c_lambda_q2": normal((N_C, C_HEAD_DIM), 0.1),
        "c_lambda_k2": normal((N_C, C_HEAD_DIM), 0.1),
        "c_subln": gain((N_C, 2 * C_HEAD_DIM)),
        "ffn_w1": normal((N_DENSE, D, D_FF), D ** -0.5),
        "ffn_w3": normal((N_DENSE, D, D_FF), D ** -0.5),
        "ffn_w2": normal((N_DENSE, D_FF, D), D_FF ** -0.5 * out_scale),
        "moe_router": normal((N_MOE, D, N_EXPERTS), D ** -0.5),
        "moe_w1": normal((N_MOE, N_EXPERTS, D, D_FF_EXPERT), D ** -0.5),
        "moe_w3": normal((N_MOE, N_EXPERTS, D, D_FF_EXPERT), D ** -0.5),
        "moe_w2": normal((N_MOE, N_EXPERTS, D_FF_EXPERT, D), D_FF_EXPERT ** -0.5 * out_scale),
    }


def reference(x_prompt, x_sample, norm_mix, norm_ffn,
              a_w_qkv, a_w_o, a_q_norm, a_k_norm,
              b_w_pw1, b_b_pw1, b_w_dw, b_b_dw, b_ln_g, b_ln_b, b_w_pw2, b_b_pw2,
              c_w_qkv, c_w_o, c_q_norm, c_k_norm,
              c_lambda_q1, c_lambda_k1, c_lambda_q2, c_lambda_k2, c_subln,
              ffn_w1, ffn_w3, ffn_w2,
              moe_router, moe_w1, moe_w3, moe_w2):
    params = dict(
        norm_mix=norm_mix, norm_ffn=norm_ffn,
        a_w_qkv=a_w_qkv, a_w_o=a_w_o, a_q_norm=a_q_norm, a_k_norm=a_k_norm,
        b_w_pw1=b_w_pw1, b_b_pw1=b_b_pw1, b_w_dw=b_w_dw, b_b_dw=b_b_dw,
        b_ln_g=b_ln_g, b_ln_b=b_ln_b, b_w_pw2=b_w_pw2, b_b_pw2=b_b_pw2,
        c_w_qkv=c_w_qkv, c_w_o=c_w_o, c_q_norm=c_q_norm, c_k_norm=c_k_norm,
        c_lambda_q1=c_lambda_q1, c_lambda_k1=c_lambda_k1,
        c_lambda_q2=c_lambda_q2, c_lambda_k2=c_lambda_k2, c_subln=c_subln,
        ffn_w1=ffn_w1, ffn_w3=ffn_w3, ffn_w2=ffn_w2,
        moe_router=moe_router, moe_w1=moe_w1, moe_w3=moe_w3, moe_w2=moe_w2,
    )
    y_prompt = trunk(x_prompt, params)
    y_sample = trunk(x_sample, params)
    return (y_prompt, y_sample)
```

```python
import functools
import math

import jax
import jax.numpy as jnp
import numpy as np
from jax import lax
from jax.experimental import pallas as pl
from jax.experimental.pallas import tpu as pltpu

EPS = 1e-6
ROPE_THETA = 10000.0
GRID_W = 64
N_MIXERS = 3
LOG2E = 1.4426950408889634

LANES = 128
V7X_VMEM_LIMIT_BYTES = 56 * 1024 * 1024

F32 = jnp.float32
BF16 = jnp.bfloat16


def _tile(dim, pref, align):
    if dim <= pref:
        return dim
    t = (pref // align) * align
    while t >= align:
        if dim % t == 0:
            return t
        t -= align
    return dim


def _params(*sem):
    return pltpu.CompilerParams(dimension_semantics=sem, vmem_limit_bytes=V7X_VMEM_LIMIT_BYTES)


def _rmsnorm_body(x_ref, g_ref, o_ref):
    x = x_ref[...]
    r = lax.rsqrt(jnp.mean(x * x, axis=-1, keepdims=True) + EPS)
    o_ref[...] = (x * r * g_ref[...]).astype(o_ref.dtype)


def rmsnorm(x, g):
    m, d = x.shape
    tm = _tile(m, 256, 8)
    return pl.pallas_call(
        _rmsnorm_body,
        out_shape=jax.ShapeDtypeStruct((m, d), BF16),
        grid=(m // tm,),
        in_specs=[pl.BlockSpec((tm, d), lambda i: (i, 0)),
                  pl.BlockSpec((1, d), lambda i: (0, 0))],
        out_specs=pl.BlockSpec((tm, d), lambda i: (i, 0)),
        compiler_params=_params("parallel"),
        name="rmsnorm",
    )(x, g.reshape(1, d))


def _rmsnorm_router_body(x_ref, g_ref, wr_ref, o_ref, gates_ref, *, n_experts):
    x = x_ref[...]
    r = lax.rsqrt(jnp.mean(x * x, axis=-1, keepdims=True) + EPS)
    h = x * r * g_ref[...]
    o_ref[...] = h.astype(o_ref.dtype)
    logits = jnp.dot(h, wr_ref[...], preferred_element_type=F32, precision=lax.Precision.HIGHEST)
    lane = lax.broadcasted_iota(jnp.int32, logits.shape, 1).astype(F32)
    neg = jnp.float32(-jnp.inf)
    lg = jnp.where(lane < n_experts, logits, neg)
    m1 = jnp.max(lg, axis=-1, keepdims=True)
    i1 = jnp.min(jnp.where(lg == m1, lane, float(LANES)), axis=-1, keepdims=True)
    lg2 = jnp.where(lane == i1, neg, lg)
    m2 = jnp.max(lg2, axis=-1, keepdims=True)
    i2 = jnp.min(jnp.where(lg2 == m2, lane, float(LANES)), axis=-1, keepdims=True)
    e2 = jnp.exp(m2 - m1)
    den = 1.0 + e2
    gates_ref[...] = jnp.where(lane == i1, 1.0 / den, 0.0) + jnp.where(lane == i2, e2 / den, 0.0)


def rmsnorm_router(x, g, w_router):
    m, d = x.shape
    e = w_router.shape[1]
    tm = _tile(m, 256, 8)
    wr = jnp.zeros((d, LANES), F32).at[:, :e].set(w_router)
    return pl.pallas_call(
        functools.partial(_rmsnorm_router_body, n_experts=e),
        out_shape=(jax.ShapeDtypeStruct((m, d), BF16), jax.ShapeDtypeStruct((m, LANES), F32)),
        grid=(m // tm,),
        in_specs=[pl.BlockSpec((tm, d), lambda i: (i, 0)),
                  pl.BlockSpec((1, d), lambda i: (0, 0)),
                  pl.BlockSpec((d, LANES), lambda i: (0, 0))],
        out_specs=(pl.BlockSpec((tm, d), lambda i: (i, 0)),
                   pl.BlockSpec((tm, LANES), lambda i: (i, 0))),
        compiler_params=_params("parallel"),
        name="rmsnorm_router",
    )(x, g.reshape(1, d), wr)


def _accumulate(acc_ref, d, k):
    @pl.when(k == 0)
    def _():
        acc_ref[...] = d

    @pl.when(k > 0)
    def _():
        acc_ref[...] += d


def _mm_plain_body(x_ref, w_ref, o_ref, acc_ref, *, nk):
    k = pl.program_id(2)
    _accumulate(acc_ref, jnp.dot(x_ref[...], w_ref[...], preferred_element_type=F32), k)

    @pl.when(k == nk - 1)
    def _():
        o_ref[...] = acc_ref[...].astype(o_ref.dtype)


def matmul(x, w, *, tm=1024, tn=1024, tk=1024):
    m, kd = x.shape
    n = w.shape[1]
    tm, tn, tk = _tile(m, tm, 8), _tile(n, tn, LANES), _tile(kd, tk, LANES)
    nk = kd // tk
    return pl.pallas_call(
        functools.partial(_mm_plain_body, nk=nk),
        out_shape=jax.ShapeDtypeStruct((m, n), BF16),
        grid=(m // tm, n // tn, nk),
        in_specs=[pl.BlockSpec((tm, tk), lambda i, j, k: (i, k)),
                  pl.BlockSpec((tk, tn), lambda i, j, k: (k, j))],
        out_specs=pl.BlockSpec((tm, tn), lambda i, j, k: (i, j)),
        scratch_shapes=[pltpu.VMEM((tm, tn), F32)],
        compiler_params=_params("parallel", "parallel", "arbitrary"),
        name="matmul",
    )(x, w)


def _mm_resid_body(*refs, nk, has_bias):
    if has_bias:
        x_ref, w_ref, r_ref, b_ref, o_ref, acc_ref = refs
    else:
        x_ref, w_ref, r_ref, o_ref, acc_ref = refs
    k = pl.program_id(2)
    _accumulate(acc_ref, jnp.dot(x_ref[...], w_ref[...], preferred_element_type=F32), k)

    @pl.when(k == nk - 1)
    def _():
        out = acc_ref[...]
        if has_bias:
            out = out + b_ref[...]
        o_ref[...] = r_ref[...] + out


def matmul_residual(x, w, resid, bias=None, *, tm=1024, tn=512, tk=1024):
    m, kd = x.shape
    n = w.shape[1]
    tm, tn, tk = _tile(m, tm, 8), _tile(n, tn, LANES), _tile(kd, tk, LANES)
    nk = kd // tk
    in_specs = [pl.BlockSpec((tm, tk), lambda i, j, k: (i, k)),
                pl.BlockSpec((tk, tn), lambda i, j, k: (k, j)),
                pl.BlockSpec((tm, tn), lambda i, j, k: (i, j))]
    args = [x, w, resid]
    if bias is not None:
        in_specs.append(pl.BlockSpec((1, tn), lambda i, j, k: (0, j)))
        args.append(bias.reshape(1, n))
    return pl.pallas_call(
        functools.partial(_mm_resid_body, nk=nk, has_bias=bias is not None),
        out_shape=jax.ShapeDtypeStruct((m, n), F32),
        grid=(m // tm, n // tn, nk),
        in_specs=in_specs,
        out_specs=pl.BlockSpec((tm, tn), lambda i, j, k: (i, j)),
        scratch_shapes=[pltpu.VMEM((tm, tn), F32)],
        compiler_params=_params("parallel", "parallel", "arbitrary"),
        name="matmul_residual",
    )(*args)


def _mm_glu_body(x_ref, wa_ref, wg_ref, ba_ref, bg_ref, o_ref, acc_a, acc_g, *, nk):
    k = pl.program_id(2)
    x = x_ref[...]
    _accumulate(acc_a, jnp.dot(x, wa_ref[...], preferred_element_type=F32), k)
    _accumulate(acc_g, jnp.dot(x, wg_ref[...], preferred_element_type=F32), k)

    @pl.when(k == nk - 1)
    def _():
        a = acc_a[...] + ba_ref[...]
        g = acc_g[...] + bg_ref[...]
        o_ref[...] = (a * jax.nn.sigmoid(g)).astype(o_ref.dtype)


def matmul_glu(x, w, b, *, tm=1024, tn=512, tk=1024):
    m, kd = x.shape
    n = w.shape[1] // 2
    tm, tn, tk = _tile(m, tm, 8), _tile(n, tn, LANES), _tile(kd, tk, LANES)
    nk, nj = kd // tk, n // tn
    b2 = b.reshape(1, 2 * n)
    return pl.pallas_call(
        functools.partial(_mm_glu_body, nk=nk),
        out_shape=jax.ShapeDtypeStruct((m, n), BF16),
        grid=(m // tm, nj, nk),
        in_specs=[pl.BlockSpec((tm, tk), lambda i, j, k: (i, k)),
                  pl.BlockSpec((tk, tn), lambda i, j, k: (k, j)),
                  pl.BlockSpec((tk, tn), lambda i, j, k: (k, j + nj)),
                  pl.BlockSpec((1, tn), lambda i, j, k: (0, j)),
                  pl.BlockSpec((1, tn), lambda i, j, k: (0, j + nj))],
        out_specs=pl.BlockSpec((tm, tn), lambda i, j, k: (i, j)),
        scratch_shapes=[pltpu.VMEM((tm, tn), F32), pltpu.VMEM((tm, tn), F32)],
        compiler_params=_params("parallel", "parallel", "arbitrary"),
        name="matmul_glu",
    )(x, w, w, b2, b2)


def _mm_swiglu_body(*refs, nk, gated):
    if gated:
        x_ref, w1_ref, w3_ref, g_ref, o_ref, acc_1, acc_3 = refs
    else:
        x_ref, w1_ref, w3_ref, o_ref, acc_1, acc_3 = refs
    k = pl.program_id(2)
    x = x_ref[...]
    _accumulate(acc_1, jnp.dot(x, w1_ref[...], preferred_element_type=F32), k)
    _accumulate(acc_3, jnp.dot(x, w3_ref[...], preferred_element_type=F32), k)

    @pl.when(k == nk - 1)
    def _():
        u = jax.nn.silu(acc_1[...]) * acc_3[...]
        if gated:
            gates = g_ref[...]
            lane = lax.broadcasted_iota(jnp.int32, gates.shape, 1)
            gate = jnp.sum(jnp.where(lane == pl.program_id(1), gates, 0.0), axis=-1, keepdims=True)
            u = u * gate
        o_ref[...] = u.astype(o_ref.dtype)


def matmul_swiglu(x, w1, w3, *, tm=1024, tn=512, tk=1024):
    m, kd = x.shape
    n = w1.shape[1]
    tm, tn, tk = _tile(m, tm, 8), _tile(n, tn, LANES), _tile(kd, tk, LANES)
    nk = kd // tk
    return pl.pallas_call(
        functools.partial(_mm_swiglu_body, nk=nk, gated=False),
        out_shape=jax.ShapeDtypeStruct((m, n), BF16),
        grid=(m // tm, n // tn, nk),
        in_specs=[pl.BlockSpec((tm, tk), lambda i, j, k: (i, k)),
                  pl.BlockSpec((tk, tn), lambda i, j, k: (k, j)),
                  pl.BlockSpec((tk, tn), lambda i, j, k: (k, j))],
        out_specs=pl.BlockSpec((tm, tn), lambda i, j, k: (i, j)),
        scratch_shapes=[pltpu.VMEM((tm, tn), F32), pltpu.VMEM((tm, tn), F32)],
        compiler_params=_params("parallel", "parallel", "arbitrary"),
        name="matmul_swiglu",
    )(x, w1, w3)


def matmul_swiglu_experts(x, w1, w3, gates, *, tm=1024, tk=1024):
    m, kd = x.shape
    e, _, f = w1.shape
    tm, tk = _tile(m, tm, 8), _tile(kd, tk, LANES)
    nk = kd // tk
    return pl.pallas_call(
        functools.partial(_mm_swiglu_body, nk=nk, gated=True),
        out_shape=jax.ShapeDtypeStruct((m, e * f), BF16),
        grid=(m // tm, e, nk),
        in_specs=[pl.BlockSpec((tm, tk), lambda i, j, k: (i, k)),
                  pl.BlockSpec((None, tk, f), lambda i, j, k: (j, k, 0)),
                  pl.BlockSpec((None, tk, f), lambda i, j, k: (j, k, 0)),
                  pl.BlockSpec((tm, LANES), lambda i, j, k: (i, 0))],
        out_specs=pl.BlockSpec((tm, f), lambda i, j, k: (i, j)),
        scratch_shapes=[pltpu.VMEM((tm, f), F32), pltpu.VMEM((tm, f), F32)],
        compiler_params=_params("parallel", "parallel", "arbitrary"),
        name="matmul_swiglu_experts",
    )(x, w1, w3, gates)


def _qk_prep_body(*refs, n_q_blocks, heads_per_block, q_scale, rope):
    if rope:
        x_ref, gq_ref, gk_ref, cos_ref, sa_ref, sb_ref, o_ref = refs
    else:
        x_ref, gq_ref, gk_ref, o_ref = refs
    is_q = pl.program_id(1) < n_q_blocks
    gain = jnp.where(is_q, gq_ref[...], gk_ref[...])
    post = jnp.where(is_q, jnp.float32(q_scale), jnp.float32(1.0))
    for h in range(heads_per_block):
        sl = slice(h * LANES, (h + 1) * LANES)
        x = x_ref[:, sl].astype(F32)
        r = lax.rsqrt(jnp.mean(x * x, axis=-1, keepdims=True) + EPS)
        y = x * r * gain
        if rope:
            y = (y * cos_ref[...] + pltpu.roll(y, LANES - LANES // 4, 1) * sa_ref[...]
                 + pltpu.roll(y, LANES // 4, 1) * sb_ref[...])
        o_ref[:, sl] = (y * post).astype(o_ref.dtype)


def qk_prep(qkv, n_q_cols, n_k_cols, g_q, g_k, q_scale, rope_tables=None):
    m = qkv.shape[0]
    hd = g_q.shape[0]
    assert hd == LANES
    tm = _tile(m, 512, 8)
    bw = _tile(math.gcd(n_q_cols, n_k_cols), 512, LANES)
    rope = rope_tables is not None
    in_specs = [pl.BlockSpec((tm, bw), lambda i, j: (i, j)),
                pl.BlockSpec((1, hd), lambda i, j: (0, 0)),
                pl.BlockSpec((1, hd), lambda i, j: (0, 0))]
    args = [qkv, g_q.reshape(1, hd), g_k.reshape(1, hd)]
    if rope:
        in_specs += [pl.BlockSpec((tm, hd), lambda i, j: (i, 0))] * 3
        args += list(rope_tables)
    return pl.pallas_call(
        functools.partial(_qk_prep_body, n_q_blocks=n_q_cols // bw, heads_per_block=bw // LANES,
                          q_scale=q_scale, rope=rope),
        out_shape=jax.ShapeDtypeStruct((m, n_q_cols + n_k_cols), BF16),
        grid=(m // tm, (n_q_cols + n_k_cols) // bw),
        in_specs=in_specs,
        out_specs=pl.BlockSpec((tm, bw), lambda i, j: (i, j)),
        compiler_params=_params("parallel", "parallel"),
        name="qk_prep",
    )(*args)


def _rope_tables(seq_lens):
    half = LANES // 2
    inv_freq = ROPE_THETA ** (-jnp.arange(0, half, 2, dtype=F32) / half)
    cos_l, sin_l = [], []
    for s in seq_lens:
        t = jnp.arange(s, dtype=jnp.int32)
        row = (t // GRID_W).astype(F32)
        col = (t % GRID_W).astype(F32)
        ang_r = row[:, None] * inv_freq[None, :]
        ang_c = col[:, None] * inv_freq[None, :]
        emb = jnp.concatenate([ang_r, ang_r, ang_c, ang_c], axis=-1)
        cos_l.append(jnp.cos(emb))
        sin_l.append(jnp.sin(emb))
    cos = jnp.concatenate(cos_l, axis=0)
    sin = jnp.concatenate(sin_l, axis=0)
    first = (jnp.arange(LANES) % half) < (half // 2)
    sin_a = jnp.where(first[None, :], -sin, 0.0)
    sin_b = jnp.where(first[None, :], 0.0, sin)
    return cos, sin_a, sin_b


def _step_tables(seq_lens, tq, tk):
    qt, kt, first, last, qrel, krel = [], [], [], [], [], []
    start = 0
    for s in seq_lens:
        for qi in range(s // tq):
            nkv = s // tk
            for ki in range(nkv):
                qt.append(start // tq + qi)
                kt.append(start // tk + ki)
                first.append(int(ki == 0))
                last.append(int(ki == nkv - 1))
                qrel.append(qi * tq)
                krel.append(ki * tk)
        start += s
    return [jnp.asarray(np.asarray(a, np.int32)) for a in (qt, kt, first, last, qrel, krel)]


def _online_softmax_step(sc, v, m_ref, l_ref, acc_ref, idx):
    m_prev = m_ref[idx]
    l_prev = l_ref[idx]
    m_new = jnp.maximum(m_prev, jnp.max(sc, axis=-1, keepdims=True))
    alpha = jnp.exp2(m_prev - m_new)
    p = jnp.exp2(sc - m_new)
    l_ref[idx] = alpha * l_prev + jnp.sum(p, axis=-1, keepdims=True)
    acc_ref[idx] = alpha * acc_ref[idx] + jnp.dot(p.astype(v.dtype), v, preferred_element_type=F32)
    m_ref[idx] = m_new


def _gqa_body(qt_ref, kt_ref, first_ref, last_ref, qrel_ref, krel_ref,
              q_ref, k_ref, v_ref, o_ref, m_ref, l_ref, acc_ref, *, group):
    s = pl.program_id(1)

    @pl.when(first_ref[s] == 1)
    def _():
        m_ref[...] = jnp.full(m_ref.shape, -jnp.inf, F32)
        l_ref[...] = jnp.zeros(l_ref.shape, F32)
        acc_ref[...] = jnp.zeros(acc_ref.shape, F32)

    k = k_ref[...]
    v = v_ref[...]
    for g in range(group):
        q = q_ref[:, g * LANES:(g + 1) * LANES]
        sc = lax.dot_general(q, k, (((1,), (1,)), ((), ())), preferred_element_type=F32)
        _online_softmax_step(sc, v, m_ref, l_ref, acc_ref, g)

    @pl.when(last_ref[s] == 1)
    def _():
        for g in range(group):
            o_ref[:, g * LANES:(g + 1) * LANES] = (acc_ref[g] / l_ref[g]).astype(o_ref.dtype)


def gqa_attention(qk, qkv, seq_lens, n_heads, n_kv, *, tq=512, tk=1024):
    m = qk.shape[0]
    group = n_heads // n_kv
    g_all = math.gcd(*seq_lens)
    tq, tk = _tile(g_all, tq, 8), _tile(g_all, tk, LANES)
    tabs = _step_tables(seq_lens, tq, tk)
    n_steps = tabs[0].shape[0]
    v_col0 = n_heads + n_kv
    return pl.pallas_call(
        functools.partial(_gqa_body, group=group),
        out_shape=jax.ShapeDtypeStruct((m, n_heads * LANES), BF16),
        grid_spec=pltpu.PrefetchScalarGridSpec(
            num_scalar_prefetch=6,
            grid=(n_kv, n_steps),
            in_specs=[pl.BlockSpec((tq, group * LANES), lambda h, s, qt, kt, *_: (qt[s], h)),
                      pl.BlockSpec((tk, LANES), lambda h, s, qt, kt, *_: (kt[s], n_heads + h)),
                      pl.BlockSpec((tk, LANES), lambda h, s, qt, kt, *_: (kt[s], v_col0 + h))],
            out_specs=pl.BlockSpec((tq, group * LANES), lambda h, s, qt, kt, *_: (qt[s], h)),
            scratch_shapes=[pltpu.VMEM((group, tq, 1), F32),
                            pltpu.VMEM((group, tq, 1), F32),
                            pltpu.VMEM((group, tq, LANES), F32)]),
        compiler_params=_params("parallel", "arbitrary"),
        name="gqa_attention",
    )(*tabs, qk, qk, qkv)


def _diff_body(qt_ref, kt_ref, first_ref, last_ref, qrel_ref, krel_ref, slope_ref,
               q_ref, k_ref, v_ref, lq1_ref, lk1_ref, lq2_ref, lk2_ref, gsub_ref,
               o_ref, m_ref, l_ref, acc_ref, *, lambda_init):
    h = pl.program_id(0)
    s = pl.program_id(1)

    @pl.when(first_ref[s] == 1)
    def _():
        m_ref[...] = jnp.full(m_ref.shape, -jnp.inf, F32)
        l_ref[...] = jnp.zeros(l_ref.shape, F32)
        acc_ref[...] = jnp.zeros(acc_ref.shape, F32)

    tq, tk = q_ref.shape[0], k_ref.shape[0]
    rows = lax.broadcasted_iota(jnp.int32, (tq, tk), 0)
    cols = lax.broadcasted_iota(jnp.int32, (tq, tk), 1)
    dist = jnp.abs(rows - cols + (qrel_ref[s] - krel_ref[s])).astype(F32)
    bias = dist * (-slope_ref[h] * LOG2E)
    v = v_ref[...]
    for c in range(2):
        sl = slice(c * LANES, (c + 1) * LANES)
        sc = lax.dot_general(q_ref[:, sl], k_ref[:, sl], (((1,), (1,)), ((), ())),
                             preferred_element_type=F32) + bias
        _online_softmax_step(sc, v, m_ref, l_ref, acc_ref, c)

    @pl.when(last_ref[s] == 1)
    def _():
        lam = (jnp.exp(jnp.sum(lq1_ref[...] * lk1_ref[...], axis=-1, keepdims=True))
               - jnp.exp(jnp.sum(lq2_ref[...] * lk2_ref[...], axis=-1, keepdims=True)) + lambda_init)
        o = acc_ref[0] * (1.0 / l_ref[0]) - acc_ref[1] * (lam * (1.0 / l_ref[1]))
        r = lax.rsqrt(jnp.mean(o * o, axis=-1, keepdims=True) + EPS)
        o_ref[...] = ((o * r * gsub_ref[...]) * (1.0 - lambda_init)).astype(o_ref.dtype)


def diff_attention(qk, qkv, seq_lens, n_heads, lq1, lk1, lq2, lk2, g_sub, lambda_init,
                   *, tq=512, tk=1024):
    m = qk.shape[0]
    dv = 2 * LANES
    g_all = math.gcd(*seq_lens)
    tq, tk = _tile(g_all, tq, 8), _tile(g_all, tk, LANES)
    tabs = _step_tables(seq_lens, tq, tk)
    n_steps = tabs[0].shape[0]
    slopes = jnp.asarray(2.0 ** (-8.0 * np.arange(1, n_heads + 1, dtype=np.float64) / n_heads), F32)
    vec = lambda a: a.reshape(1, -1)
    small = pl.BlockSpec((1, LANES), lambda h, s, *_: (0, 0))
    return pl.pallas_call(
        functools.partial(_diff_body, lambda_init=lambda_init),
        out_shape=jax.ShapeDtypeStruct((m, n_heads * dv), BF16),
        grid_spec=pltpu.PrefetchScalarGridSpec(
            num_scalar_prefetch=7,
            grid=(n_heads, n_steps),
            in_specs=[pl.BlockSpec((tq, dv), lambda h, s, qt, kt, *_: (qt[s], h)),
                      pl.BlockSpec((tk, dv), lambda h, s, qt, kt, *_: (kt[s], n_heads + h)),
                      pl.BlockSpec((tk, dv), lambda h, s, qt, kt, *_: (kt[s], 2 * n_heads + h)),
                      small, small, small, small,
                      pl.BlockSpec((1, dv), lambda h, s, *_: (0, 0))],
            out_specs=pl.BlockSpec((tq, dv), lambda h, s, qt, kt, *_: (qt[s], h)),
            scratch_shapes=[pltpu.VMEM((2, tq, 1), F32),
                            pltpu.VMEM((2, tq, 1), F32),
                            pltpu.VMEM((2, tq, dv), F32)]),
        compiler_params=_params("parallel", "arbitrary"),
        name="diff_attention",
    )(*tabs, slopes, qk, qk, qkv, vec(lq1), vec(lk1), vec(lq2), vec(lk2), vec(g_sub))


HALO = 16


def _dwconv_body(hp_ref, hn_ref, prev_ref, cur_ref, next_ref, w_ref, b_ref, o_ref, win_ref, *, width):
    i = pl.program_id(0)
    tt = cur_ref.shape[0]
    pad = width // 2
    zeros = jnp.zeros(prev_ref.shape, F32)
    win_ref[0:HALO, :] = jnp.where(hp_ref[i] == 1, prev_ref[...].astype(F32), zeros)
    win_ref[HALO:HALO + tt, :] = cur_ref[...].astype(F32)
    win_ref[HALO + tt:, :] = jnp.where(hn_ref[i] == 1, next_ref[...].astype(F32), zeros)
    rc = 32
    for r0 in range(0, tt, rc):
        acc = jnp.zeros((rc, cur_ref.shape[1]), F32) + b_ref[...]
        for t in range(width):
            off = HALO - pad + t + r0
            acc = acc + win_ref[off:off + rc, :] * w_ref[t:t + 1, :]
        o_ref[r0:r0 + rc, :] = acc.astype(o_ref.dtype)


def dwconv(u, w_dw, b_dw, seq_lens, *, tt=256, tc=512):
    m, d = u.shape
    width = w_dw.shape[0]
    assert width // 2 <= HALO
    tt = _tile(math.gcd(*seq_lens), tt, 32)
    tc = _tile(d, tc, LANES)
    nb = tt // HALO
    starts = np.cumsum([0] + list(seq_lens))
    has_prev = np.ones(m // tt, np.int32)
    has_next = np.ones(m // tt, np.int32)
    for st in starts[:-1]:
        has_prev[st // tt] = 0
    for en in starts[1:]:
        has_next[en // tt - 1] = 0
    last_halo = m // HALO - 1
    return pl.pallas_call(
        functools.partial(_dwconv_body, width=width),
        out_shape=jax.ShapeDtypeStruct((m, d), F32),
        grid_spec=pltpu.PrefetchScalarGridSpec(
            num_scalar_prefetch=2,
            grid=(m // tt, d // tc),
            in_specs=[pl.BlockSpec((HALO, tc), lambda i, j, *_: (jnp.maximum(i * nb - 1, 0), j)),
                      pl.BlockSpec((tt, tc), lambda i, j, *_: (i, j)),
                      pl.BlockSpec((HALO, tc), lambda i, j, *_: (jnp.minimum((i + 1) * nb, last_halo), j)),
                      pl.BlockSpec((width, tc), lambda i, j, *_: (0, j)),
                      pl.BlockSpec((1, tc), lambda i, j, *_: (0, j))],
            out_specs=pl.BlockSpec((tt, tc), lambda i, j, *_: (i, j)),
            scratch_shapes=[pltpu.VMEM((tt + 2 * HALO, tc), F32)]),
        compiler_params=_params("parallel", "parallel"),
        name="dwconv",
    )(jnp.asarray(has_prev), jnp.asarray(has_next), u, u, u, w_dw, b_dw.reshape(1, d))


def _ln_silu_body(x_ref, g_ref, b_ref, o_ref):
    x = x_ref[...]
    mu = jnp.mean(x, axis=-1, keepdims=True)
    xc = x - mu
    var = jnp.mean(xc * xc, axis=-1, keepdims=True)
    y = xc * lax.rsqrt(var + EPS) * g_ref[...] + b_ref[...]
    o_ref[...] = jax.nn.silu(y).astype(o_ref.dtype)


def layernorm_silu(x, g, b):
    m, d = x.shape
    tm = _tile(m, 256, 8)
    return pl.pallas_call(
        _ln_silu_body,
        out_shape=jax.ShapeDtypeStruct((m, d), BF16),
        grid=(m // tm,),
        in_specs=[pl.BlockSpec((tm, d), lambda i: (i, 0)),
                  pl.BlockSpec((1, d), lambda i: (0, 0)),
                  pl.BlockSpec((1, d), lambda i: (0, 0))],
        out_specs=pl.BlockSpec((tm, d), lambda i: (i, 0)),
        compiler_params=_params("parallel"),
        name="layernorm_silu",
    )(x, g.reshape(1, d), b.reshape(1, d))


def kernel(x_prompt, x_sample, norm_mix, norm_ffn, a_w_qkv, a_w_o, a_q_norm, a_k_norm, b_w_pw1, b_b_pw1, b_w_dw, b_b_dw, b_ln_g, b_ln_b, b_w_pw2, b_b_pw2, c_w_qkv, c_w_o, c_q_norm, c_k_norm, c_lambda_q1, c_lambda_k1, c_lambda_q2, c_lambda_k2, c_subln, ffn_w1, ffn_w3, ffn_w2, moe_router, moe_w1, moe_w3, moe_w2):
    depth, d = norm_mix.shape
    bp, sp, _ = x_prompt.shape
    bs, ss, _ = x_sample.shape
    seq_lens = [sp] * bp + [ss] * bs
    mp = bp * sp
    x = jnp.concatenate([x_prompt.reshape(mp, d), x_sample.reshape(bs * ss, d)], axis=0)

    hd = a_q_norm.shape[-1]
    a_heads = d // hd
    a_kv = (a_w_qkv.shape[-1] - d) // (2 * hd)
    c_heads = d // (2 * c_q_norm.shape[-1])
    n_exp, _, f_exp = moe_w1.shape[1:]
    rope_tables = _rope_tables(seq_lens)
    bf = lambda w: w.astype(BF16)

    for i in range(depth):
        h = rmsnorm(x, norm_mix[i])
        mixer, j = i % N_MIXERS, i // N_MIXERS
        if mixer == 0:
            qkv = matmul(h, bf(a_w_qkv[j]))
            qk = qk_prep(qkv, d, a_kv * hd, a_q_norm[j], a_k_norm[j], hd ** -0.5 * LOG2E, rope_tables)
            o = gqa_attention(qk, qkv, seq_lens, a_heads, a_kv)
            x = matmul_residual(o, bf(a_w_o[j]), x)
        elif mixer == 1:
            u = matmul_glu(h, bf(b_w_pw1[j]), b_b_pw1[j])
            u = dwconv(u, b_w_dw[j], b_b_dw[j], seq_lens)
            u = layernorm_silu(u, b_ln_g[j], b_ln_b[j])
            x = matmul_residual(u, bf(b_w_pw2[j]), x, b_b_pw2[j])
        else:
            lambda_init = 0.8 - 0.6 * math.exp(-0.3 * i)
            qkv = matmul(h, bf(c_w_qkv[j]))
            qk = qk_prep(qkv, d, d, c_q_norm[j], c_k_norm[j], c_q_norm.shape[-1] ** -0.5 * LOG2E)
            o = diff_attention(qk, qkv, seq_lens, c_heads, c_lambda_q1[j], c_lambda_k1[j],
                               c_lambda_q2[j], c_lambda_k2[j], c_subln[j], lambda_init)
            x = matmul_residual(o, bf(c_w_o[j]), x)
        k = i // 2
        if i % 2 == 0:
            h = rmsnorm(x, norm_ffn[i])
            u = matmul_swiglu(h, bf(ffn_w1[k]), bf(ffn_w3[k]))
            x = matmul_residual(u, bf(ffn_w2[k]), x)
        else:
            h, gates = rmsnorm_router(x, norm_ffn[i], moe_router[k])
            u = matmul_swiglu_experts(h, bf(moe_w1[k]), bf(moe_w3[k]), gates)
            x = matmul_residual(u, bf(moe_w2[k]).reshape(n_exp * f_exp, d), x)

    return (x[:mp].reshape(bp, sp, d), x[mp:].reshape(bs, ss, d))
```

```python
import functools
import math

import jax
import jax.numpy as jnp
import numpy as np
from jax import lax
from jax.experimental import pallas as pl
from jax.experimental.pallas import tpu as pltpu

EPS = 1e-6
ROPE_THETA = 10000.0
GRID_W = 64
N_MIXERS = 3
LOG2E = 1.4426950408889634

LANES = 128
SUBLANES = 8
V7X_MXU_DIM = 256
V7X_VMEM_LIMIT_BYTES = 56 * 1024 * 1024

F32 = jnp.float32
BF16 = jnp.bfloat16


def _tile(dim, pref, align):
    if dim <= pref:
        return dim
    t = (pref // align) * align
    while t >= align:
        if dim % t == 0:
            return t
        t -= align
    return dim


def _params(*sem):
    return pltpu.CompilerParams(dimension_semantics=sem, vmem_limit_bytes=V7X_VMEM_LIMIT_BYTES)


def _rmsnorm_body(x_ref, g_ref, o_ref):
    x = x_ref[...]
    r = lax.rsqrt(jnp.mean(x * x, axis=-1, keepdims=True) + EPS)
    o_ref[...] = (x * r * g_ref[...]).astype(o_ref.dtype)


def rmsnorm(x, g):
    m, d = x.shape
    tm = _tile(m, 256, 8)
    return pl.pallas_call(
        _rmsnorm_body,
        out_shape=jax.ShapeDtypeStruct((m, d), BF16),
        grid=(m // tm,),
        in_specs=[pl.BlockSpec((tm, d), lambda i: (i, 0)),
                  pl.BlockSpec((1, d), lambda i: (0, 0))],
        out_specs=pl.BlockSpec((tm, d), lambda i: (i, 0)),
        compiler_params=_params("parallel"),
        name="rmsnorm",
    )(x, g.reshape(1, d))


def _rmsnorm_router_body(x_ref, g_ref, wr_ref, o_ref, gates_ref, *, n_experts):
    x = x_ref[...]
    r = lax.rsqrt(jnp.mean(x * x, axis=-1, keepdims=True) + EPS)
    h = x * r * g_ref[...]
    o_ref[...] = h.astype(o_ref.dtype)
    logits = jnp.dot(h, wr_ref[...], preferred_element_type=F32, precision=lax.Precision.HIGHEST)
    lane = lax.broadcasted_iota(jnp.int32, logits.shape, 1).astype(F32)
    neg = jnp.float32(-jnp.inf)
    lg = jnp.where(lane < n_experts, logits, neg)
    m1 = jnp.max(lg, axis=-1, keepdims=True)
    i1 = jnp.min(jnp.where(lg == m1, lane, float(LANES)), axis=-1, keepdims=True)
    lg2 = jnp.where(lane == i1, neg, lg)
    m2 = jnp.max(lg2, axis=-1, keepdims=True)
    i2 = jnp.min(jnp.where(lg2 == m2, lane, float(LANES)), axis=-1, keepdims=True)
    e2 = jnp.exp(m2 - m1)
    den = 1.0 + e2
    gates_ref[...] = jnp.where(lane == i1, 1.0 / den, 0.0) + jnp.where(lane == i2, e2 / den, 0.0)


def rmsnorm_router(x, g, w_router):
    m, d = x.shape
    e = w_router.shape[1]
    tm = _tile(m, 256, 8)
    wr = jnp.zeros((d, LANES), F32).at[:, :e].set(w_router)
    return pl.pallas_call(
        functools.partial(_rmsnorm_router_body, n_experts=e),
        out_shape=(jax.ShapeDtypeStruct((m, d), BF16), jax.ShapeDtypeStruct((m, LANES), F32)),
        grid=(m // tm,),
        in_specs=[pl.BlockSpec((tm, d), lambda i: (i, 0)),
                  pl.BlockSpec((1, d), lambda i: (0, 0)),
                  pl.BlockSpec((d, LANES), lambda i: (0, 0))],
        out_specs=(pl.BlockSpec((tm, d), lambda i: (i, 0)),
                   pl.BlockSpec((tm, LANES), lambda i: (i, 0))),
        compiler_params=_params("parallel"),
        name="rmsnorm_router",
    )(x, g.reshape(1, d), wr)


def _matmul_steps(nk, dots, acc_refs, finish):
    if nk == 1:
        finish(dots)
        return
    k = pl.program_id(2)

    @pl.when(k == 0)
    def _():
        for acc in acc_refs:
            acc[...] = jnp.zeros(acc.shape, F32)

    for acc, d in zip(acc_refs, dots):
        acc[...] += d

    @pl.when(k == nk - 1)
    def _():
        finish([acc[...] for acc in acc_refs])


def _mm_tiles(m, n, kd, tm, tn, tk=4096):
    tk = _tile(kd, tk, V7X_MXU_DIM)
    return _tile(m, tm, SUBLANES), _tile(n, tn, LANES), tk, kd // tk


def _acc_scratch(nk, count, tm, tn):
    return [pltpu.VMEM((tm, tn), F32)] * count if nk > 1 else []


def _mm_plain_body(x_ref, w_ref, o_ref, *acc, nk):
    def finish(s):
        o_ref[...] = s[0].astype(o_ref.dtype)

    _matmul_steps(nk, [jnp.dot(x_ref[...], w_ref[...], preferred_element_type=F32)], acc, finish)


def matmul(x, w, *, tm=1024, tn=1024):
    m, kd = x.shape
    n = w.shape[1]
    tm, tn, tk, nk = _mm_tiles(m, n, kd, tm, tn)
    return pl.pallas_call(
        functools.partial(_mm_plain_body, nk=nk),
        out_shape=jax.ShapeDtypeStruct((m, n), BF16),
        grid=(m // tm, n // tn, nk),
        in_specs=[pl.BlockSpec((tm, tk), lambda i, j, k: (i, k)),
                  pl.BlockSpec((tk, tn), lambda i, j, k: (k, j))],
        out_specs=pl.BlockSpec((tm, tn), lambda i, j, k: (i, j)),
        scratch_shapes=_acc_scratch(nk, 1, tm, tn),
        compiler_params=_params("parallel", "parallel", "arbitrary"),
        name="matmul",
    )(x, w)


def _mm_resid_body(*refs, nk, has_bias):
    x_ref, w_ref, r_ref = refs[:3]
    b_ref = refs[3] if has_bias else None
    o_ref = refs[3 + has_bias]
    acc = refs[4 + has_bias:]

    def finish(s):
        out = s[0] + b_ref[...] if has_bias else s[0]
        o_ref[...] = r_ref[...] + out

    _matmul_steps(nk, [jnp.dot(x_ref[...], w_ref[...], preferred_element_type=F32)], acc, finish)


def matmul_residual(x, w, resid, bias=None, *, tm=1024, tn=512):
    m, kd = x.shape
    n = w.shape[1]
    if kd > 4096:
        tm = tm // 2
    tm, tn, tk, nk = _mm_tiles(m, n, kd, tm, tn, tk=8192)
    in_specs = [pl.BlockSpec((tm, tk), lambda i, j, k: (i, k)),
                pl.BlockSpec((tk, tn), lambda i, j, k: (k, j)),
                pl.BlockSpec((tm, tn), lambda i, j, k: (i, j))]
    args = [x, w, resid]
    if bias is not None:
        in_specs.append(pl.BlockSpec((1, tn), lambda i, j, k: (0, j)))
        args.append(bias.reshape(1, n))
    return pl.pallas_call(
        functools.partial(_mm_resid_body, nk=nk, has_bias=bias is not None),
        out_shape=jax.ShapeDtypeStruct((m, n), F32),
        grid=(m // tm, n // tn, nk),
        in_specs=in_specs,
        out_specs=pl.BlockSpec((tm, tn), lambda i, j, k: (i, j)),
        scratch_shapes=_acc_scratch(nk, 1, tm, tn),
        compiler_params=_params("parallel", "parallel", "arbitrary"),
        name="matmul_residual",
    )(*args)


def _mm_glu_body(x_ref, wa_ref, wg_ref, ba_ref, bg_ref, o_ref, *acc, nk):
    def finish(s):
        a = s[0] + ba_ref[...]
        g = s[1] + bg_ref[...]
        o_ref[...] = (a * jax.nn.sigmoid(g)).astype(o_ref.dtype)

    x = x_ref[...]
    _matmul_steps(nk, [jnp.dot(x, wa_ref[...], preferred_element_type=F32),
                       jnp.dot(x, wg_ref[...], preferred_element_type=F32)], acc, finish)


def matmul_glu(x, w, b, *, tm=1024, tn=512):
    m, kd = x.shape
    n = w.shape[1] // 2
    tm, tn, tk, nk = _mm_tiles(m, n, kd, tm, tn)
    nj = n // tn
    b2 = b.reshape(1, 2 * n)
    return pl.pallas_call(
        functools.partial(_mm_glu_body, nk=nk),
        out_shape=jax.ShapeDtypeStruct((m, n), BF16),
        grid=(m // tm, nj, nk),
        in_specs=[pl.BlockSpec((tm, tk), lambda i, j, k: (i, k)),
                  pl.BlockSpec((tk, tn), lambda i, j, k: (k, j)),
                  pl.BlockSpec((tk, tn), lambda i, j, k: (k, j + nj)),
                  pl.BlockSpec((1, tn), lambda i, j, k: (0, j)),
                  pl.BlockSpec((1, tn), lambda i, j, k: (0, j + nj))],
        out_specs=pl.BlockSpec((tm, tn), lambda i, j, k: (i, j)),
        scratch_shapes=_acc_scratch(nk, 2, tm, tn),
        compiler_params=_params("parallel", "parallel", "arbitrary"),
        name="matmul_glu",
    )(x, w, w, b2, b2)


def _mm_swiglu_body(*refs, nk, gated):
    x_ref, w1_ref, w3_ref = refs[:3]
    g_ref = refs[3] if gated else None
    o_ref = refs[3 + gated]
    acc = refs[4 + gated:]

    def finish(s):
        u = jax.nn.silu(s[0]) * s[1]
        if gated:
            gates = g_ref[...]
            lane = lax.broadcasted_iota(jnp.int32, gates.shape, 1)
            gate = jnp.sum(jnp.where(lane == pl.program_id(1), gates, 0.0), axis=-1, keepdims=True)
            u = u * gate
        o_ref[...] = u.astype(o_ref.dtype)

    x = x_ref[...]
    _matmul_steps(nk, [jnp.dot(x, w1_ref[...], preferred_element_type=F32),
                       jnp.dot(x, w3_ref[...], preferred_element_type=F32)], acc, finish)


def matmul_swiglu(x, w1, w3, *, tm=1024, tn=512):
    m, kd = x.shape
    n = w1.shape[1]
    tm, tn, tk, nk = _mm_tiles(m, n, kd, tm, tn)
    return pl.pallas_call(
        functools.partial(_mm_swiglu_body, nk=nk, gated=False),
        out_shape=jax.ShapeDtypeStruct((m, n), BF16),
        grid=(m // tm, n // tn, nk),
        in_specs=[pl.BlockSpec((tm, tk), lambda i, j, k: (i, k)),
                  pl.BlockSpec((tk, tn), lambda i, j, k: (k, j)),
                  pl.BlockSpec((tk, tn), lambda i, j, k: (k, j))],
        out_specs=pl.BlockSpec((tm, tn), lambda i, j, k: (i, j)),
        scratch_shapes=_acc_scratch(nk, 2, tm, tn),
        compiler_params=_params("parallel", "parallel", "arbitrary"),
        name="matmul_swiglu",
    )(x, w1, w3)


def matmul_swiglu_experts(x, w1, w3, gates, *, tm=1024):
    m, kd = x.shape
    e, _, f = w1.shape
    tm, _, tk, nk = _mm_tiles(m, f, kd, tm, f)
    return pl.pallas_call(
        functools.partial(_mm_swiglu_body, nk=nk, gated=True),
        out_shape=jax.ShapeDtypeStruct((m, e * f), BF16),
        grid=(m // tm, e, nk),
        in_specs=[pl.BlockSpec((tm, tk), lambda i, j, k: (i, k)),
                  pl.BlockSpec((None, tk, f), lambda i, j, k: (j, k, 0)),
                  pl.BlockSpec((None, tk, f), lambda i, j, k: (j, k, 0)),
                  pl.BlockSpec((tm, LANES), lambda i, j, k: (i, 0))],
        out_specs=pl.BlockSpec((tm, f), lambda i, j, k: (i, j)),
        scratch_shapes=_acc_scratch(nk, 2, tm, f),
        compiler_params=_params("parallel", "parallel", "arbitrary"),
        name="matmul_swiglu_experts",
    )(x, w1, w3, gates)


def _qk_prep_body(*refs, n_q_blocks, heads_per_block, q_scale, rope):
    if rope:
        x_ref, gq_ref, gk_ref, cos_ref, sa_ref, sb_ref, o_ref = refs
    else:
        x_ref, gq_ref, gk_ref, o_ref = refs
    is_q = pl.program_id(1) < n_q_blocks
    gain = jnp.where(is_q, gq_ref[...], gk_ref[...])
    post = jnp.where(is_q, jnp.float32(q_scale), jnp.float32(1.0))
    for h in range(heads_per_block):
        sl = slice(h * LANES, (h + 1) * LANES)
        x = x_ref[:, sl].astype(F32)
        r = lax.rsqrt(jnp.mean(x * x, axis=-1, keepdims=True) + EPS)
        y = x * r * gain
        if rope:
            y = (y * cos_ref[...] + pltpu.roll(y, LANES - LANES // 4, 1) * sa_ref[...]
                 + pltpu.roll(y, LANES // 4, 1) * sb_ref[...])
        o_ref[:, sl] = (y * post).astype(o_ref.dtype)


def qk_prep(qkv, n_q_cols, n_k_cols, g_q, g_k, q_scale, rope_tables=None):
    m = qkv.shape[0]
    hd = g_q.shape[0]
    assert hd == LANES
    tm = _tile(m, 512, 8)
    bw = _tile(math.gcd(n_q_cols, n_k_cols), 512, LANES)
    rope = rope_tables is not None
    in_specs = [pl.BlockSpec((tm, bw), lambda i, j: (i, j)),
                pl.BlockSpec((1, hd), lambda i, j: (0, 0)),
                pl.BlockSpec((1, hd), lambda i, j: (0, 0))]
    args = [qkv, g_q.reshape(1, hd), g_k.reshape(1, hd)]
    if rope:
        in_specs += [pl.BlockSpec((tm, hd), lambda i, j: (i, 0))] * 3
        args += list(rope_tables)
    return pl.pallas_call(
        functools.partial(_qk_prep_body, n_q_blocks=n_q_cols // bw, heads_per_block=bw // LANES,
                          q_scale=q_scale, rope=rope),
        out_shape=jax.ShapeDtypeStruct((m, n_q_cols + n_k_cols), BF16),
        grid=(m // tm, (n_q_cols + n_k_cols) // bw),
        in_specs=in_specs,
        out_specs=pl.BlockSpec((tm, bw), lambda i, j: (i, j)),
        compiler_params=_params("parallel", "parallel"),
        name="qk_prep",
    )(*args)


def _rope_tables(seq_lens):
    half = LANES // 2
    inv_freq = ROPE_THETA ** (-jnp.arange(0, half, 2, dtype=F32) / half)
    cos_l, sin_l = [], []
    for s in seq_lens:
        t = jnp.arange(s, dtype=jnp.int32)
        row = (t // GRID_W).astype(F32)
        col = (t % GRID_W).astype(F32)
        ang_r = row[:, None] * inv_freq[None, :]
        ang_c = col[:, None] * inv_freq[None, :]
        emb = jnp.concatenate([ang_r, ang_r, ang_c, ang_c], axis=-1)
        cos_l.append(jnp.cos(emb))
        sin_l.append(jnp.sin(emb))
    cos = jnp.concatenate(cos_l, axis=0)
    sin = jnp.concatenate(sin_l, axis=0)
    first = (jnp.arange(LANES) % half) < (half // 2)
    sin_a = jnp.where(first[None, :], -sin, 0.0)
    sin_b = jnp.where(first[None, :], 0.0, sin)
    return cos, sin_a, sin_b


def _transpose_body(x_ref, o_ref):
    o_ref[...] = x_ref[...].astype(F32).T.astype(o_ref.dtype)


def transpose_cols(x, col0, ncols, *, tm=512, bw=512):
    m = x.shape[0]
    tm = _tile(m, tm, LANES)
    bw = _tile(math.gcd(ncols, col0) if col0 else ncols, bw, LANES)
    c0 = col0 // bw
    return pl.pallas_call(
        _transpose_body,
        out_shape=jax.ShapeDtypeStruct((ncols, m), x.dtype),
        grid=(m // tm, ncols // bw),
        in_specs=[pl.BlockSpec((tm, bw), lambda i, j: (i, c0 + j))],
        out_specs=pl.BlockSpec((bw, tm), lambda i, j: (j, i)),
        compiler_params=_params("parallel", "parallel"),
        name="transpose_cols",
    )(x)


def _step_tables(seq_lens, tq, tk):
    qt, kt, first, last, qrel, krel, side = [], [], [], [], [], [], []
    start = 0
    for s in seq_lens:
        for qi in range(s // tq):
            nkv = s // tk
            for ki in range(nkv):
                qt.append(start // tq + qi)
                kt.append(start // tk + ki)
                first.append(int(ki == 0))
                last.append(int(ki == nkv - 1))
                qrel.append(qi * tq)
                krel.append(ki * tk)
                side.append(1 if (ki + 1) * tk <= qi * tq else (-1 if ki * tk >= (qi + 1) * tq else 0))
        start += s
    return [jnp.asarray(np.asarray(a, np.int32)) for a in (qt, kt, first, last, qrel, krel, side)]


ONES_ROWS = 16


def _with_ones_rows(vt):
    return jnp.concatenate([vt, jnp.ones((ONES_ROWS, vt.shape[1]), vt.dtype)], axis=0)


def _online_softmax_step(st, vt1, m_ref, acc_ref, idx, shift=None):
    m_prev = m_ref[idx]
    tile_max = jnp.max(st, axis=0, keepdims=True)
    if shift is not None:
        tile_max = tile_max + shift
    m_new = jnp.maximum(m_prev, tile_max)
    alpha = jnp.exp2(m_prev - m_new)
    p = jnp.exp2(st - (m_new if shift is None else m_new - shift))
    acc_ref[idx] = alpha * acc_ref[idx] + jnp.dot(vt1, p.astype(vt1.dtype), preferred_element_type=F32)
    m_ref[idx] = m_new


def _scores_t(k, q):
    return lax.dot_general(k, q, (((1,), (1,)), ((), ())), preferred_element_type=F32)


def _chunk_pipeline(n_chunks, scores, softmax):
    blocks = scores(0)
    for c in range(n_chunks):
        ahead = scores(c + 1) if c + 1 < n_chunks else None
        softmax(c, blocks)
        blocks = ahead


def _init_softmax_state(m_ref, acc_ref):
    m_ref[...] = jnp.full(m_ref.shape, -jnp.inf, F32)
    acc_ref[...] = jnp.zeros(acc_ref.shape, F32)


def _gqa_body(qt_ref, kt_ref, first_ref, last_ref, qrel_ref, krel_ref, side_ref,
              q_ref, k_ref, vt_ref, o_ref, m_ref, acc_ref, *, group, kc):
    s = pl.program_id(1)

    @pl.when(first_ref[s] == 1)
    def _():
        _init_softmax_state(m_ref, acc_ref)

    def scores(c):
        k = k_ref[c * kc:(c + 1) * kc, :]
        return [_scores_t(k, q_ref[:, g * LANES:(g + 1) * LANES]) for g in range(group)]

    def softmax(c, sts):
        vt1 = _with_ones_rows(vt_ref[:, c * kc:(c + 1) * kc])
        for g in range(group):
            _online_softmax_step(sts[g], vt1, m_ref, acc_ref, g)

    _chunk_pipeline(k_ref.shape[0] // kc, scores, softmax)

    @pl.when(last_ref[s] == 1)
    def _():
        for g in range(group):
            acc = acc_ref[g]
            o = acc[:LANES] / acc[LANES:LANES + 1]
            o_ref[:, g * LANES:(g + 1) * LANES] = o.T.astype(o_ref.dtype)


def gqa_attention(qk, vt, seq_lens, n_heads, n_kv, *, tq=512, tk=1024, kc=512):
    m = qk.shape[0]
    group = n_heads // n_kv
    g_all = math.gcd(*seq_lens)
    tq, tk = _tile(g_all, tq, LANES), _tile(g_all, tk, LANES)
    tabs = _step_tables(seq_lens, tq, tk)
    n_steps = tabs[0].shape[0]
    return pl.pallas_call(
        functools.partial(_gqa_body, group=group, kc=_tile(tk, kc, LANES)),
        out_shape=jax.ShapeDtypeStruct((m, n_heads * LANES), BF16),
        grid_spec=pltpu.PrefetchScalarGridSpec(
            num_scalar_prefetch=7,
            grid=(n_kv, n_steps),
            in_specs=[pl.BlockSpec((tq, group * LANES), lambda h, s, qt, kt, *_: (qt[s], h)),
                      pl.BlockSpec((tk, LANES), lambda h, s, qt, kt, *_: (kt[s], n_heads + h)),
                      pl.BlockSpec((LANES, tk), lambda h, s, qt, kt, *_: (h, kt[s]))],
            out_specs=pl.BlockSpec((tq, group * LANES), lambda h, s, qt, kt, *_: (qt[s], h)),
            scratch_shapes=[pltpu.VMEM((group, 1, tq), F32),
                            pltpu.VMEM((group, LANES + ONES_ROWS, tq), F32)]),
        compiler_params=_params("parallel", "arbitrary"),
        name="gqa_attention",
    )(*tabs, qk, qk, vt)


ALIBI_SPLIT = 3


def _diff_body(qt_ref, kt_ref, first_ref, last_ref, qrel_ref, krel_ref, side_ref, sgn_ref, slope_ref,
               q_ref, k_ref, vt_ref, kpos_ref, ext_ref, lq1_ref, lk1_ref, lq2_ref, lk2_ref, gsub_ref,
               o_ref, m_ref, acc_ref, *, lambda_init, kc):
    h = pl.program_id(0)
    s = pl.program_id(1)

    @pl.when(first_ref[s] == 1)
    def _():
        _init_softmax_state(m_ref, acc_ref)

    tq, tk = q_ref.shape[0], k_ref.shape[0]
    dv = vt_ref.shape[0]
    slope2 = slope_ref[h]
    subs = [slice(c * LANES, (c + 1) * LANES) for c in range(2)]
    rows = lambda c: slice(c * kc, (c + 1) * kc)

    def softmax(c, sts, shift=None):
        vt1 = _with_ones_rows(vt_ref[:, rows(c)])
        for sub in range(2):
            _online_softmax_step(sts[sub], vt1, m_ref, acc_ref, sub, shift)

    @pl.when(side_ref[s] == 0)
    def _():
        keys = lax.broadcasted_iota(jnp.int32, (kc, tq), 0)
        queries = lax.broadcasted_iota(jnp.int32, (kc, tq), 1)
        delta = keys - queries + (krel_ref[s] - qrel_ref[s])

        def scores(c):
            bias = jnp.abs(delta + c * kc).astype(F32) * (-slope2)
            return [_scores_t(k_ref[rows(c), sl], q_ref[:, sl]) + bias for sl in subs]

        _chunk_pipeline(tk // kc, scores, softmax)

    @pl.when(side_ref[s] != 0)
    def _():
        sgn = sgn_ref[s]
        qpos = qrel_ref[s] + lax.broadcasted_iota(jnp.int32, (1, tq), 1)
        shift = (sgn * slope2) * (krel_ref[s] - qpos).astype(F32)
        q_ext = jnp.broadcast_to(ext_ref[...] * sgn, (tq, LANES)).astype(q_ref.dtype)
        q_aug = [jnp.concatenate([q_ref[:, sl], q_ext], axis=1) for sl in subs]

        def scores(c):
            k_ext = kpos_ref[rows(c), :]
            return [_scores_t(jnp.concatenate([k_ref[rows(c), sl], k_ext], axis=1), q_aug[sub])
                    for sub, sl in enumerate(subs)]

        _chunk_pipeline(tk // kc, scores, functools.partial(softmax, shift=shift))

    @pl.when(last_ref[s] == 1)
    def _():
        lam = (jnp.exp(jnp.sum(lq1_ref[...] * lk1_ref[...], axis=-1, keepdims=True))
               - jnp.exp(jnp.sum(lq2_ref[...] * lk2_ref[...], axis=-1, keepdims=True)) + lambda_init)
        acc0, acc1 = acc_ref[0], acc_ref[1]
        o = (acc0[:dv] * (1.0 / acc0[dv:dv + 1])
             - acc1[:dv] * (lam * (1.0 / acc1[dv:dv + 1])))
        r = lax.rsqrt(jnp.mean(o * o, axis=0, keepdims=True) + EPS)
        o_ref[...] = (((o * r).T * gsub_ref[...]) * (1.0 - lambda_init)).astype(o_ref.dtype)


def _alibi_tables(n_heads, tk):
    slope2 = (2.0 ** (-8.0 * np.arange(1, n_heads + 1, dtype=np.float64) / n_heads) * LOG2E).astype(np.float32)
    to_bf16 = lambda a: a.astype(BF16).astype(np.float32)
    pieces, rest = [], slope2.copy()
    for _ in range(ALIBI_SPLIT):
        c = to_bf16(rest)
        pieces.append(c)
        rest = (rest - c).astype(np.float32)
    ext = np.zeros((n_heads, 1, LANES), np.float32)
    kpos = np.zeros((tk, LANES), np.float32)
    j = np.arange(tk)
    for i, c in enumerate(pieces):
        ext[:, 0, i] = float(LANES) * c
        ext[:, 0, ALIBI_SPLIT + i] = c
        kpos[:, i] = j // LANES
        kpos[:, ALIBI_SPLIT + i] = j % LANES
    assert tk // LANES <= 256, "key offsets must stay exact in bf16"
    return jnp.asarray(slope2), jnp.asarray(ext), jnp.asarray(kpos, BF16)


def diff_attention(qk, vt, seq_lens, n_heads, lq1, lk1, lq2, lk2, g_sub, lambda_init,
                   *, tq=1024, tk=1024, kc=256):
    m = qk.shape[0]
    dv = 2 * LANES
    g_all = math.gcd(*seq_lens)
    tq, tk = _tile(g_all, tq, LANES), _tile(g_all, tk, LANES)
    tabs = _step_tables(seq_lens, tq, tk)
    n_steps = tabs[0].shape[0]
    sgn = tabs[-1].astype(F32)
    slope2, ext, kpos = _alibi_tables(n_heads, tk)
    vec = lambda a: a.reshape(1, -1)
    small = pl.BlockSpec((1, LANES), lambda h, s, *_: (0, 0))
    return pl.pallas_call(
        functools.partial(_diff_body, lambda_init=lambda_init, kc=_tile(tk, kc, LANES)),
        out_shape=jax.ShapeDtypeStruct((m, n_heads * dv), BF16),
        grid_spec=pltpu.PrefetchScalarGridSpec(
            num_scalar_prefetch=9,
            grid=(n_heads, n_steps),
            in_specs=[pl.BlockSpec((tq, dv), lambda h, s, qt, kt, *_: (qt[s], h)),
                      pl.BlockSpec((tk, dv), lambda h, s, qt, kt, *_: (kt[s], n_heads + h)),
                      pl.BlockSpec((dv, tk), lambda h, s, qt, kt, *_: (h, kt[s])),
                      pl.BlockSpec((tk, LANES), lambda h, s, *_: (0, 0)),
                      pl.BlockSpec((None, 1, LANES), lambda h, s, *_: (h, 0, 0)),
                      small, small, small, small,
                      pl.BlockSpec((1, dv), lambda h, s, *_: (0, 0))],
            out_specs=pl.BlockSpec((tq, dv), lambda h, s, qt, kt, *_: (qt[s], h)),
            scratch_shapes=[pltpu.VMEM((2, 1, tq), F32),
                            pltpu.VMEM((2, dv + ONES_ROWS, tq), F32)]),
        compiler_params=_params("parallel", "arbitrary"),
        name="diff_attention",
    )(*tabs, sgn, slope2, qk, qk, vt, kpos, ext, vec(lq1), vec(lk1), vec(lq2), vec(lk2), vec(g_sub))


HALO = 16


def _dwconv_body(hp_ref, hn_ref, prev_ref, cur_ref, next_ref, w_ref, b_ref, o_ref, win_ref, *, width):
    i = pl.program_id(0)
    tt = cur_ref.shape[0]
    pad = width // 2
    zeros = jnp.zeros(prev_ref.shape, F32)
    win_ref[0:HALO, :] = jnp.where(hp_ref[i] == 1, prev_ref[...].astype(F32), zeros)
    win_ref[HALO:HALO + tt, :] = cur_ref[...].astype(F32)
    win_ref[HALO + tt:, :] = jnp.where(hn_ref[i] == 1, next_ref[...].astype(F32), zeros)
    rc = 64
    first = HALO - pad
    span = rc + SUBLANES * ((width - 1) // SUBLANES) + SUBLANES
    for r0 in range(0, tt, rc):
        acc = jnp.zeros((rc, cur_ref.shape[1]), F32) + b_ref[...]
        block = win_ref[r0:r0 + span, :]
        for phase in range(SUBLANES):
            shift = (first + phase) % SUBLANES
            rolled = block if shift == 0 else pltpu.roll(block, span - shift, 0)
            for t in range(phase, width, SUBLANES):
                a = first + t - shift
                acc = acc + rolled[a:a + rc, :] * w_ref[t:t + 1, :]
        o_ref[r0:r0 + rc, :] = acc.astype(o_ref.dtype)


def dwconv(u, w_dw, b_dw, seq_lens, *, tt=256, tc=512):
    m, d = u.shape
    width = w_dw.shape[0]
    assert width // 2 <= HALO
    tt = _tile(math.gcd(*seq_lens), tt, 64)
    tc = _tile(d, tc, LANES)
    nb = tt // HALO
    starts = np.cumsum([0] + list(seq_lens))
    has_prev = np.ones(m // tt, np.int32)
    has_next = np.ones(m // tt, np.int32)
    for st in starts[:-1]:
        has_prev[st // tt] = 0
    for en in starts[1:]:
        has_next[en // tt - 1] = 0
    last_halo = m // HALO - 1
    return pl.pallas_call(
        functools.partial(_dwconv_body, width=width),
        out_shape=jax.ShapeDtypeStruct((m, d), F32),
        grid_spec=pltpu.PrefetchScalarGridSpec(
            num_scalar_prefetch=2,
            grid=(m // tt, d // tc),
            in_specs=[pl.BlockSpec((HALO, tc), lambda i, j, *_: (jnp.maximum(i * nb - 1, 0), j)),
                      pl.BlockSpec((tt, tc), lambda i, j, *_: (i, j)),
                      pl.BlockSpec((HALO, tc), lambda i, j, *_: (jnp.minimum((i + 1) * nb, last_halo), j)),
                      pl.BlockSpec((width, tc), lambda i, j, *_: (0, j)),
                      pl.BlockSpec((1, tc), lambda i, j, *_: (0, j))],
            out_specs=pl.BlockSpec((tt, tc), lambda i, j, *_: (i, j)),
            scratch_shapes=[pltpu.VMEM((tt + 2 * HALO, tc), F32)]),
        compiler_params=_params("parallel", "parallel"),
        name="dwconv",
    )(jnp.asarray(has_prev), jnp.asarray(has_next), u, u, u, w_dw, b_dw.reshape(1, d))


def _ln_silu_body(x_ref, g_ref, b_ref, o_ref):
    x = x_ref[...]
    mu = jnp.mean(x, axis=-1, keepdims=True)
    xc = x - mu
    var = jnp.mean(xc * xc, axis=-1, keepdims=True)
    y = xc * lax.rsqrt(var + EPS) * g_ref[...] + b_ref[...]
    o_ref[...] = jax.nn.silu(y).astype(o_ref.dtype)


def layernorm_silu(x, g, b):
    m, d = x.shape
    tm = _tile(m, 256, 8)
    return pl.pallas_call(
        _ln_silu_body,
        out_shape=jax.ShapeDtypeStruct((m, d), BF16),
        grid=(m // tm,),
        in_specs=[pl.BlockSpec((tm, d), lambda i: (i, 0)),
                  pl.BlockSpec((1, d), lambda i: (0, 0)),
                  pl.BlockSpec((1, d), lambda i: (0, 0))],
        out_specs=pl.BlockSpec((tm, d), lambda i: (i, 0)),
        compiler_params=_params("parallel"),
        name="layernorm_silu",
    )(x, g.reshape(1, d), b.reshape(1, d))


def kernel(x_prompt, x_sample, norm_mix, norm_ffn, a_w_qkv, a_w_o, a_q_norm, a_k_norm, b_w_pw1, b_b_pw1, b_w_dw, b_b_dw, b_ln_g, b_ln_b, b_w_pw2, b_b_pw2, c_w_qkv, c_w_o, c_q_norm, c_k_norm, c_lambda_q1, c_lambda_k1, c_lambda_q2, c_lambda_k2, c_subln, ffn_w1, ffn_w3, ffn_w2, moe_router, moe_w1, moe_w3, moe_w2):
    depth, d = norm_mix.shape
    bp, sp, _ = x_prompt.shape
    bs, ss, _ = x_sample.shape
    seq_lens = [sp] * bp + [ss] * bs
    mp = bp * sp
    x = jnp.concatenate([x_prompt.reshape(mp, d), x_sample.reshape(bs * ss, d)], axis=0)

    hd = a_q_norm.shape[-1]
    a_heads = d // hd
    a_kv = (a_w_qkv.shape[-1] - d) // (2 * hd)
    c_heads = d // (2 * c_q_norm.shape[-1])
    n_exp, _, f_exp = moe_w1.shape[1:]
    rope_tables = _rope_tables(seq_lens)
    bf = lambda w: w.astype(BF16)

    for i in range(depth):
        h = rmsnorm(x, norm_mix[i])
        mixer, j = i % N_MIXERS, i // N_MIXERS
        if mixer == 0:
            qkv = matmul(h, bf(a_w_qkv[j]))
            qk = qk_prep(qkv, d, a_kv * hd, a_q_norm[j], a_k_norm[j], hd ** -0.5 * LOG2E, rope_tables)
            vt = transpose_cols(qkv, d + a_kv * hd, a_kv * hd)
            o = gqa_attention(qk, vt, seq_lens, a_heads, a_kv)
            x = matmul_residual(o, bf(a_w_o[j]), x)
        elif mixer == 1:
            u = matmul_glu(h, bf(b_w_pw1[j]), b_b_pw1[j])
            u = dwconv(u, b_w_dw[j], b_b_dw[j], seq_lens)
            u = layernorm_silu(u, b_ln_g[j], b_ln_b[j])
            x = matmul_residual(u, bf(b_w_pw2[j]), x, b_b_pw2[j])
        else:
            lambda_init = 0.8 - 0.6 * math.exp(-0.3 * i)
            qkv = matmul(h, bf(c_w_qkv[j]))
            qk = qk_prep(qkv, d, d, c_q_norm[j], c_k_norm[j], c_q_norm.shape[-1] ** -0.5 * LOG2E)
            vt = transpose_cols(qkv, 2 * d, d)
            o = diff_attention(qk, vt, seq_lens, c_heads, c_lambda_q1[j], c_lambda_k1[j],
                               c_lambda_q2[j], c_lambda_k2[j], c_subln[j], lambda_init)
            x = matmul_residual(o, bf(c_w_o[j]), x)
        k = i // 2
        if i % 2 == 0:
            h = rmsnorm(x, norm_ffn[i])
            u = matmul_swiglu(h, bf(ffn_w1[k]), bf(ffn_w3[k]))
            x = matmul_residual(u, bf(ffn_w2[k]), x)
        else:
            h, gates = rmsnorm_router(x, norm_ffn[i], moe_router[k])
            u = matmul_swiglu_experts(h, bf(moe_w1[k]), bf(moe_w3[k]), gates)
            x = matmul_residual(u, bf(moe_w2[k]).reshape(n_exp * f_exp, d), x)

    return (x[:mp].reshape(bp, sp, d), x[mp:].reshape(bs, ss, d))
```

```python
import functools
import math

import jax
import jax.numpy as jnp
import numpy as np
from jax import lax
from jax.experimental import pallas as pl
from jax.experimental.pallas import tpu as pltpu

EPS = 1e-6
ROPE_THETA = 10000.0
GRID_W = 64
N_MIXERS = 3
LOG2E = 1.4426950408889634

LANES = 128
SUBLANES = 8
V7X_MXU_DIM = 256
V7X_VMEM_LIMIT_BYTES = 56 * 1024 * 1024

F32 = jnp.float32
BF16 = jnp.bfloat16


def _tile(dim, pref, align):
    if dim <= pref:
        return dim
    t = (pref // align) * align
    while t >= align:
        if dim % t == 0:
            return t
        t -= align
    return dim


def _params(*sem):
    return pltpu.CompilerParams(dimension_semantics=sem, vmem_limit_bytes=V7X_VMEM_LIMIT_BYTES)


def _rmsnorm_body(x_ref, g_ref, o_ref):
    x = x_ref[...]
    r = lax.rsqrt(jnp.mean(x * x, axis=-1, keepdims=True) + EPS)
    o_ref[...] = (x * r * g_ref[...]).astype(o_ref.dtype)


def rmsnorm(x, g):
    m, d = x.shape
    tm = _tile(m, 256, 8)
    return pl.pallas_call(
        _rmsnorm_body,
        out_shape=jax.ShapeDtypeStruct((m, d), BF16),
        grid=(m // tm,),
        in_specs=[pl.BlockSpec((tm, d), lambda i: (i, 0)),
                  pl.BlockSpec((1, d), lambda i: (0, 0))],
        out_specs=pl.BlockSpec((tm, d), lambda i: (i, 0)),
        compiler_params=_params("parallel"),
        name="rmsnorm",
    )(x, g.reshape(1, d))


def _rmsnorm_router_body(x_ref, g_ref, wr_ref, o_ref, gates_ref, *, n_experts):
    x = x_ref[...]
    r = lax.rsqrt(jnp.mean(x * x, axis=-1, keepdims=True) + EPS)
    h = x * r * g_ref[...]
    o_ref[...] = h.astype(o_ref.dtype)
    logits = jnp.dot(h, wr_ref[...], preferred_element_type=F32, precision=lax.Precision.HIGHEST)
    lane = lax.broadcasted_iota(jnp.int32, logits.shape, 1).astype(F32)
    neg = jnp.float32(-jnp.inf)
    lg = jnp.where(lane < n_experts, logits, neg)
    m1 = jnp.max(lg, axis=-1, keepdims=True)
    i1 = jnp.min(jnp.where(lg == m1, lane, float(LANES)), axis=-1, keepdims=True)
    lg2 = jnp.where(lane == i1, neg, lg)
    m2 = jnp.max(lg2, axis=-1, keepdims=True)
    i2 = jnp.min(jnp.where(lg2 == m2, lane, float(LANES)), axis=-1, keepdims=True)
    e2 = jnp.exp(m2 - m1)
    den = 1.0 + e2
    gates_ref[...] = jnp.where(lane == i1, 1.0 / den, 0.0) + jnp.where(lane == i2, e2 / den, 0.0)


def rmsnorm_router(x, g, w_router):
    m, d = x.shape
    e = w_router.shape[1]
    tm = _tile(m, 256, 8)
    wr = jnp.zeros((d, LANES), F32).at[:, :e].set(w_router)
    return pl.pallas_call(
        functools.partial(_rmsnorm_router_body, n_experts=e),
        out_shape=(jax.ShapeDtypeStruct((m, d), BF16), jax.ShapeDtypeStruct((m, LANES), F32)),
        grid=(m // tm,),
        in_specs=[pl.BlockSpec((tm, d), lambda i: (i, 0)),
                  pl.BlockSpec((1, d), lambda i: (0, 0)),
                  pl.BlockSpec((d, LANES), lambda i: (0, 0))],
        out_specs=(pl.BlockSpec((tm, d), lambda i: (i, 0)),
                   pl.BlockSpec((tm, LANES), lambda i: (i, 0))),
        compiler_params=_params("parallel"),
        name="rmsnorm_router",
    )(x, g.reshape(1, d), wr)


def _matmul_steps(nk, dots, acc_refs, finish):
    if nk == 1:
        finish(dots)
        return
    k = pl.program_id(2)

    @pl.when(k == 0)
    def _():
        for acc in acc_refs:
            acc[...] = jnp.zeros(acc.shape, F32)

    for acc, d in zip(acc_refs, dots):
        acc[...] += d

    @pl.when(k == nk - 1)
    def _():
        finish([acc[...] for acc in acc_refs])


def _mm_tiles(m, n, kd, tm, tn, tk=4096):
    tk = _tile(kd, tk, V7X_MXU_DIM)
    return _tile(m, tm, SUBLANES), _tile(n, tn, LANES), tk, kd // tk


def _acc_scratch(nk, count, tm, tn):
    return [pltpu.VMEM((tm, tn), F32)] * count if nk > 1 else []


def _mm_plain_body(x_ref, w_ref, o_ref, *acc, nk):
    def finish(s):
        o_ref[...] = s[0].astype(o_ref.dtype)

    _matmul_steps(nk, [jnp.dot(x_ref[...], w_ref[...], preferred_element_type=F32)], acc, finish)


def matmul(x, w, *, tm=1024, tn=1024):
    m, kd = x.shape
    n = w.shape[1]
    tm, tn, tk, nk = _mm_tiles(m, n, kd, tm, tn)
    return pl.pallas_call(
        functools.partial(_mm_plain_body, nk=nk),
        out_shape=jax.ShapeDtypeStruct((m, n), BF16),
        grid=(m // tm, n // tn, nk),
        in_specs=[pl.BlockSpec((tm, tk), lambda i, j, k: (i, k)),
                  pl.BlockSpec((tk, tn), lambda i, j, k: (k, j))],
        out_specs=pl.BlockSpec((tm, tn), lambda i, j, k: (i, j)),
        scratch_shapes=_acc_scratch(nk, 1, tm, tn),
        compiler_params=_params("parallel", "parallel", "arbitrary"),
        name="matmul",
    )(x, w)


def _mm_resid_body(*refs, nk, has_bias):
    x_ref, w_ref, r_ref = refs[:3]
    b_ref = refs[3] if has_bias else None
    o_ref = refs[3 + has_bias]
    acc = refs[4 + has_bias:]

    def finish(s):
        out = s[0] + b_ref[...] if has_bias else s[0]
        o_ref[...] = r_ref[...] + out

    _matmul_steps(nk, [jnp.dot(x_ref[...], w_ref[...], preferred_element_type=F32)], acc, finish)


def matmul_residual(x, w, resid, bias=None, *, tm=1024, tn=512):
    m, kd = x.shape
    n = w.shape[1]
    if kd > 4096:
        tm = tm // 2
    tm, tn, tk, nk = _mm_tiles(m, n, kd, tm, tn, tk=8192)
    in_specs = [pl.BlockSpec((tm, tk), lambda i, j, k: (i, k)),
                pl.BlockSpec((tk, tn), lambda i, j, k: (k, j)),
                pl.BlockSpec((tm, tn), lambda i, j, k: (i, j))]
    args = [x, w, resid]
    if bias is not None:
        in_specs.append(pl.BlockSpec((1, tn), lambda i, j, k: (0, j)))
        args.append(bias.reshape(1, n))
    return pl.pallas_call(
        functools.partial(_mm_resid_body, nk=nk, has_bias=bias is not None),
        out_shape=jax.ShapeDtypeStruct((m, n), F32),
        grid=(m // tm, n // tn, nk),
        in_specs=in_specs,
        out_specs=pl.BlockSpec((tm, tn), lambda i, j, k: (i, j)),
        scratch_shapes=_acc_scratch(nk, 1, tm, tn),
        compiler_params=_params("parallel", "parallel", "arbitrary"),
        name="matmul_residual",
    )(*args)


def _mm_glu_body(x_ref, wa_ref, wg_ref, ba_ref, bg_ref, o_ref, *acc, nk):
    def finish(s):
        a = s[0] + ba_ref[...]
        g = s[1] + bg_ref[...]
        o_ref[...] = (a * jax.nn.sigmoid(g)).astype(o_ref.dtype)

    x = x_ref[...]
    _matmul_steps(nk, [jnp.dot(x, wa_ref[...], preferred_element_type=F32),
                       jnp.dot(x, wg_ref[...], preferred_element_type=F32)], acc, finish)


def matmul_glu(x, w, b, *, tm=1024, tn=512):
    m, kd = x.shape
    n = w.shape[1] // 2
    tm, tn, tk, nk = _mm_tiles(m, n, kd, tm, tn)
    nj = n // tn
    b2 = b.reshape(1, 2 * n)
    return pl.pallas_call(
        functools.partial(_mm_glu_body, nk=nk),
        out_shape=jax.ShapeDtypeStruct((m, n), BF16),
        grid=(m // tm, nj, nk),
        in_specs=[pl.BlockSpec((tm, tk), lambda i, j, k: (i, k)),
                  pl.BlockSpec((tk, tn), lambda i, j, k: (k, j)),
                  pl.BlockSpec((tk, tn), lambda i, j, k: (k, j + nj)),
                  pl.BlockSpec((1, tn), lambda i, j, k: (0, j)),
                  pl.BlockSpec((1, tn), lambda i, j, k: (0, j + nj))],
        out_specs=pl.BlockSpec((tm, tn), lambda i, j, k: (i, j)),
        scratch_shapes=_acc_scratch(nk, 2, tm, tn),
        compiler_params=_params("parallel", "parallel", "arbitrary"),
        name="matmul_glu",
    )(x, w, w, b2, b2)


def _mm_swiglu_body(*refs, nk, gated):
    x_ref, w1_ref, w3_ref = refs[:3]
    g_ref = refs[3] if gated else None
    o_ref = refs[3 + gated]
    acc = refs[4 + gated:]

    def finish(s):
        u = jax.nn.silu(s[0]) * s[1]
        if gated:
            gates = g_ref[...]
            lane = lax.broadcasted_iota(jnp.int32, gates.shape, 1)
            gate = jnp.sum(jnp.where(lane == pl.program_id(1), gates, 0.0), axis=-1, keepdims=True)
            u = u * gate
        o_ref[...] = u.astype(o_ref.dtype)

    x = x_ref[...]
    _matmul_steps(nk, [jnp.dot(x, w1_ref[...], preferred_element_type=F32),
                       jnp.dot(x, w3_ref[...], preferred_element_type=F32)], acc, finish)


def matmul_swiglu(x, w1, w3, *, tm=1024, tn=512):
    m, kd = x.shape
    n = w1.shape[1]
    tm, tn, tk, nk = _mm_tiles(m, n, kd, tm, tn)
    return pl.pallas_call(
        functools.partial(_mm_swiglu_body, nk=nk, gated=False),
        out_shape=jax.ShapeDtypeStruct((m, n), BF16),
        grid=(m // tm, n // tn, nk),
        in_specs=[pl.BlockSpec((tm, tk), lambda i, j, k: (i, k)),
                  pl.BlockSpec((tk, tn), lambda i, j, k: (k, j)),
                  pl.BlockSpec((tk, tn), lambda i, j, k: (k, j))],
        out_specs=pl.BlockSpec((tm, tn), lambda i, j, k: (i, j)),
        scratch_shapes=_acc_scratch(nk, 2, tm, tn),
        compiler_params=_params("parallel", "parallel", "arbitrary"),
        name="matmul_swiglu",
    )(x, w1, w3)


def matmul_swiglu_experts(x, w1, w3, gates, *, tm=1024):
    m, kd = x.shape
    e, _, f = w1.shape
    tm, _, tk, nk = _mm_tiles(m, f, kd, tm, f)
    return pl.pallas_call(
        functools.partial(_mm_swiglu_body, nk=nk, gated=True),
        out_shape=jax.ShapeDtypeStruct((m, e * f), BF16),
        grid=(m // tm, e, nk),
        in_specs=[pl.BlockSpec((tm, tk), lambda i, j, k: (i, k)),
                  pl.BlockSpec((None, tk, f), lambda i, j, k: (j, k, 0)),
                  pl.BlockSpec((None, tk, f), lambda i, j, k: (j, k, 0)),
                  pl.BlockSpec((tm, LANES), lambda i, j, k: (i, 0))],
        out_specs=pl.BlockSpec((tm, f), lambda i, j, k: (i, j)),
        scratch_shapes=_acc_scratch(nk, 2, tm, f),
        compiler_params=_params("parallel", "parallel", "arbitrary"),
        name="matmul_swiglu_experts",
    )(x, w1, w3, gates)


def _qk_prep_body(*refs, n_q_blocks, heads_per_block, q_scale, rope):
    if rope:
        x_ref, gq_ref, gk_ref, cos_ref, sa_ref, sb_ref, o_ref = refs
    else:
        x_ref, gq_ref, gk_ref, o_ref = refs
    is_q = pl.program_id(1) < n_q_blocks
    gain = jnp.where(is_q, gq_ref[...], gk_ref[...])
    post = jnp.where(is_q, jnp.float32(q_scale), jnp.float32(1.0))
    for h in range(heads_per_block):
        sl = slice(h * LANES, (h + 1) * LANES)
        x = x_ref[:, sl].astype(F32)
        r = lax.rsqrt(jnp.mean(x * x, axis=-1, keepdims=True) + EPS)
        y = x * r * gain
        if rope:
            y = (y * cos_ref[...] + pltpu.roll(y, LANES - LANES // 4, 1) * sa_ref[...]
                 + pltpu.roll(y, LANES // 4, 1) * sb_ref[...])
        o_ref[:, sl] = (y * post).astype(o_ref.dtype)


def qk_prep(qkv, n_q_cols, n_k_cols, g_q, g_k, q_scale, rope_tables=None):
    m = qkv.shape[0]
    hd = g_q.shape[0]
    assert hd == LANES
    tm = _tile(m, 512, 8)
    bw = _tile(math.gcd(n_q_cols, n_k_cols), 512, LANES)
    rope = rope_tables is not None
    in_specs = [pl.BlockSpec((tm, bw), lambda i, j: (i, j)),
                pl.BlockSpec((1, hd), lambda i, j: (0, 0)),
                pl.BlockSpec((1, hd), lambda i, j: (0, 0))]
    args = [qkv, g_q.reshape(1, hd), g_k.reshape(1, hd)]
    if rope:
        in_specs += [pl.BlockSpec((tm, hd), lambda i, j: (i, 0))] * 3
        args += list(rope_tables)
    return pl.pallas_call(
        functools.partial(_qk_prep_body, n_q_blocks=n_q_cols // bw, heads_per_block=bw // LANES,
                          q_scale=q_scale, rope=rope),
        out_shape=jax.ShapeDtypeStruct((m, n_q_cols + n_k_cols), BF16),
        grid=(m // tm, (n_q_cols + n_k_cols) // bw),
        in_specs=in_specs,
        out_specs=pl.BlockSpec((tm, bw), lambda i, j: (i, j)),
        compiler_params=_params("parallel", "parallel"),
        name="qk_prep",
    )(*args)


def _rope_tables(seq_lens):
    half = LANES // 2
    inv_freq = ROPE_THETA ** (-jnp.arange(0, half, 2, dtype=F32) / half)
    cos_l, sin_l = [], []
    for s in seq_lens:
        t = jnp.arange(s, dtype=jnp.int32)
        row = (t // GRID_W).astype(F32)
        col = (t % GRID_W).astype(F32)
        ang_r = row[:, None] * inv_freq[None, :]
        ang_c = col[:, None] * inv_freq[None, :]
        emb = jnp.concatenate([ang_r, ang_r, ang_c, ang_c], axis=-1)
        cos_l.append(jnp.cos(emb))
        sin_l.append(jnp.sin(emb))
    cos = jnp.concatenate(cos_l, axis=0)
    sin = jnp.concatenate(sin_l, axis=0)
    first = (jnp.arange(LANES) % half) < (half // 2)
    sin_a = jnp.where(first[None, :], -sin, 0.0)
    sin_b = jnp.where(first[None, :], 0.0, sin)
    return cos, sin_a, sin_b


def _transpose_body(x_ref, o_ref):
    o_ref[...] = x_ref[...].astype(F32).T.astype(o_ref.dtype)


def transpose_cols(x, col0, ncols, *, tm=512, bw=512):
    m = x.shape[0]
    tm = _tile(m, tm, LANES)
    bw = _tile(math.gcd(ncols, col0) if col0 else ncols, bw, LANES)
    c0 = col0 // bw
    return pl.pallas_call(
        _transpose_body,
        out_shape=jax.ShapeDtypeStruct((ncols, m), x.dtype),
        grid=(m // tm, ncols // bw),
        in_specs=[pl.BlockSpec((tm, bw), lambda i, j: (i, c0 + j))],
        out_specs=pl.BlockSpec((bw, tm), lambda i, j: (j, i)),
        compiler_params=_params("parallel", "parallel"),
        name="transpose_cols",
    )(x)


def _step_tables(seq_lens, tq, tk):
    qt, kt, first, last, qrel, krel, side = [], [], [], [], [], [], []
    start = 0
    for s in seq_lens:
        for qi in range(s // tq):
            nkv = s // tk
            for ki in range(nkv):
                qt.append(start // tq + qi)
                kt.append(start // tk + ki)
                first.append(int(ki == 0))
                last.append(int(ki == nkv - 1))
                qrel.append(qi * tq)
                krel.append(ki * tk)
                side.append(1 if (ki + 1) * tk <= qi * tq else (-1 if ki * tk >= (qi + 1) * tq else 0))
        start += s
    return [jnp.asarray(np.asarray(a, np.int32)) for a in (qt, kt, first, last, qrel, krel, side)]


ONES_ROWS = 16


def _with_ones_rows(vt):
    return jnp.concatenate([vt, jnp.ones((ONES_ROWS, vt.shape[1]), vt.dtype)], axis=0)


def _online_softmax_step(st, vt1, m_ref, acc_ref, idx, shift=None):
    m_prev = m_ref[idx]
    tile_max = jnp.max(st, axis=0, keepdims=True)
    if shift is not None:
        tile_max = tile_max + shift
    m_new = jnp.maximum(m_prev, tile_max)
    alpha = jnp.exp2(m_prev - m_new)
    p = jnp.exp2(st - (m_new if shift is None else m_new - shift))
    acc_ref[idx] = alpha * acc_ref[idx] + jnp.dot(vt1, p.astype(vt1.dtype), preferred_element_type=F32)
    m_ref[idx] = m_new


SAFE_LOG2 = 60.0


def _score_bound(g_q, g_k, q_scale):
    bound = q_scale * LANES * jnp.max(jnp.abs(g_q)) * jnp.max(jnp.abs(g_k))
    return (bound <= SAFE_LOG2).astype(jnp.int32).reshape(1)


def _pv_fixed_reference(st, vt1):
    return jnp.dot(vt1, jnp.exp2(st).astype(vt1.dtype), preferred_element_type=F32)


def _scores_t(k, q):
    return lax.dot_general(k, q, (((1,), (1,)), ((), ())), preferred_element_type=F32)


def _chunk_pipeline(n_chunks, scores, softmax):
    blocks = scores(0)
    for c in range(n_chunks):
        ahead = scores(c + 1) if c + 1 < n_chunks else None
        softmax(c, blocks)
        blocks = ahead


def _init_softmax_state(m_ref, acc_ref):
    m_ref[...] = jnp.full(m_ref.shape, -jnp.inf, F32)
    acc_ref[...] = jnp.zeros(acc_ref.shape, F32)


def _gqa_body(qt_ref, kt_ref, first_ref, last_ref, qrel_ref, krel_ref, side_ref, bounded_ref,
              q_ref, k_ref, vt_ref, o_ref, m_ref, acc_ref, *, group, kc):
    s = pl.program_id(1)

    @pl.when(first_ref[s] == 1)
    def _():
        _init_softmax_state(m_ref, acc_ref)

    def scores(c):
        k = k_ref[c * kc:(c + 1) * kc, :]
        return [_scores_t(k, q_ref[:, g * LANES:(g + 1) * LANES]) for g in range(group)]

    @pl.when(bounded_ref[0] == 1)
    def _():
        pv = [[] for _ in range(group)]

        def softmax(c, sts):
            vt1 = _with_ones_rows(vt_ref[:, c * kc:(c + 1) * kc])
            for g in range(group):
                pv[g].append(_pv_fixed_reference(sts[g], vt1))

        _chunk_pipeline(k_ref.shape[0] // kc, scores, softmax)
        for g in range(group):
            acc_ref[g] += functools.reduce(lambda a, b: a + b, pv[g])

    @pl.when(bounded_ref[0] == 0)
    def _():
        def softmax(c, sts):
            vt1 = _with_ones_rows(vt_ref[:, c * kc:(c + 1) * kc])
            for g in range(group):
                _online_softmax_step(sts[g], vt1, m_ref, acc_ref, g)

        _chunk_pipeline(k_ref.shape[0] // kc, scores, softmax)

    @pl.when(last_ref[s] == 1)
    def _():
        for g in range(group):
            acc = acc_ref[g]
            o = acc[:LANES] / acc[LANES:LANES + 1]
            o_ref[:, g * LANES:(g + 1) * LANES] = o.T.astype(o_ref.dtype)


def gqa_attention(qk, vt, bounded, seq_lens, n_heads, n_kv, *, tq=512, tk=2048, kc=256):
    m = qk.shape[0]
    group = n_heads // n_kv
    g_all = math.gcd(*seq_lens)
    tq, tk = _tile(g_all, tq, LANES), _tile(g_all, tk, LANES)
    tabs = _step_tables(seq_lens, tq, tk)
    n_steps = tabs[0].shape[0]
    return pl.pallas_call(
        functools.partial(_gqa_body, group=group, kc=_tile(tk, kc, LANES)),
        out_shape=jax.ShapeDtypeStruct((m, n_heads * LANES), BF16),
        grid_spec=pltpu.PrefetchScalarGridSpec(
            num_scalar_prefetch=8,
            grid=(n_kv, n_steps),
            in_specs=[pl.BlockSpec((tq, group * LANES), lambda h, s, qt, kt, *_: (qt[s], h)),
                      pl.BlockSpec((tk, LANES), lambda h, s, qt, kt, *_: (kt[s], n_heads + h)),
                      pl.BlockSpec((LANES, tk), lambda h, s, qt, kt, *_: (h, kt[s]))],
            out_specs=pl.BlockSpec((tq, group * LANES), lambda h, s, qt, kt, *_: (qt[s], h)),
            scratch_shapes=[pltpu.VMEM((group, 1, tq), F32),
                            pltpu.VMEM((group, LANES + ONES_ROWS, tq), F32)]),
        compiler_params=_params("parallel", "arbitrary"),
        name="gqa_attention",
    )(*tabs, bounded, qk, qk, vt)


ALIBI_SPLIT = 3


def _diff_body(qt_ref, kt_ref, first_ref, last_ref, qrel_ref, krel_ref, side_ref, bounded_ref, sgn_ref, slope_ref,
               q_ref, k_ref, vt_ref, kpos_ref, ext_ref, lq1_ref, lk1_ref, lq2_ref, lk2_ref, gsub_ref,
               o_ref, m_ref, acc_ref, *, lambda_init, kc):
    h = pl.program_id(0)
    s = pl.program_id(1)

    @pl.when(first_ref[s] == 1)
    def _():
        _init_softmax_state(m_ref, acc_ref)

    tq, tk = q_ref.shape[0], k_ref.shape[0]
    dv = vt_ref.shape[0]
    slope2 = slope_ref[h]
    subs = [slice(c * LANES, (c + 1) * LANES) for c in range(2)]
    rows = lambda c: slice(c * kc, (c + 1) * kc)

    def online(scores, shift=None):
        def softmax(c, sts):
            vt1 = _with_ones_rows(vt_ref[:, rows(c)])
            for sub in range(2):
                _online_softmax_step(sts[sub], vt1, m_ref, acc_ref, sub, shift)

        _chunk_pipeline(tk // kc, scores, softmax)

    def fixed_reference(scores):
        pv = [[], []]

        def softmax(c, sts):
            vt1 = _with_ones_rows(vt_ref[:, rows(c)])
            for sub in range(2):
                pv[sub].append(_pv_fixed_reference(sts[sub], vt1))

        _chunk_pipeline(tk // kc, scores, softmax)
        for sub in range(2):
            acc_ref[sub] += functools.reduce(lambda a, b: a + b, pv[sub])

    bounded = bounded_ref[0] == 1

    @pl.when(side_ref[s] == 0)
    def _():
        keys = lax.broadcasted_iota(jnp.int32, (kc, tq), 0)
        queries = lax.broadcasted_iota(jnp.int32, (kc, tq), 1)
        delta = keys - queries + (krel_ref[s] - qrel_ref[s])

        def scores(c):
            bias = jnp.abs(delta + c * kc).astype(F32) * (-slope2)
            return [_scores_t(k_ref[rows(c), sl], q_ref[:, sl]) + bias for sl in subs]

        pl.when(bounded)(lambda: fixed_reference(scores))
        pl.when(jnp.logical_not(bounded))(lambda: online(scores))

    @pl.when(side_ref[s] != 0)
    def _():
        sgn = sgn_ref[s]
        ext_row = ext_ref[...] * sgn

        def augmented(q_ext):
            q_aug = [jnp.concatenate([q_ref[:, sl], q_ext.astype(q_ref.dtype)], axis=1) for sl in subs]

            def scores(c):
                k_ext = kpos_ref[rows(c), :]
                return [_scores_t(jnp.concatenate([k_ref[rows(c), sl], k_ext], axis=1), q_aug[sub])
                        for sub, sl in enumerate(subs)]

            return scores

        @pl.when(bounded)
        def _():
            qpos = qrel_ref[s] + lax.broadcasted_iota(jnp.int32, (tq, LANES), 0)
            rest = (sgn * slope2) * (krel_ref[s] - qpos).astype(F32)
            lane = lax.broadcasted_iota(jnp.int32, (tq, LANES), 1)
            q_ext = jnp.broadcast_to(ext_row, (tq, LANES))
            for i in range(ALIBI_SPLIT):
                piece = rest.astype(BF16).astype(F32)
                q_ext = jnp.where(lane == 2 * ALIBI_SPLIT + i, piece, q_ext)
                rest = rest - piece
            fixed_reference(augmented(q_ext))

        @pl.when(jnp.logical_not(bounded))
        def _():
            qpos = qrel_ref[s] + lax.broadcasted_iota(jnp.int32, (1, tq), 1)
            shift = (sgn * slope2) * (krel_ref[s] - qpos).astype(F32)
            online(augmented(jnp.broadcast_to(ext_row, (tq, LANES))), shift)

    @pl.when(last_ref[s] == 1)
    def _():
        lam = (jnp.exp(jnp.sum(lq1_ref[...] * lk1_ref[...], axis=-1, keepdims=True))
               - jnp.exp(jnp.sum(lq2_ref[...] * lk2_ref[...], axis=-1, keepdims=True)) + lambda_init)
        acc0, acc1 = acc_ref[0], acc_ref[1]
        o = (acc0[:dv] * (1.0 / acc0[dv:dv + 1])
             - acc1[:dv] * (lam * (1.0 / acc1[dv:dv + 1])))
        r = lax.rsqrt(jnp.mean(o * o, axis=0, keepdims=True) + EPS)
        o_ref[...] = (((o * r).T * gsub_ref[...]) * (1.0 - lambda_init)).astype(o_ref.dtype)


def _alibi_tables(n_heads, tk):
    slope2 = (2.0 ** (-8.0 * np.arange(1, n_heads + 1, dtype=np.float64) / n_heads) * LOG2E).astype(np.float32)
    to_bf16 = lambda a: a.astype(BF16).astype(np.float32)
    pieces, rest = [], slope2.copy()
    for _ in range(ALIBI_SPLIT):
        c = to_bf16(rest)
        pieces.append(c)
        rest = (rest - c).astype(np.float32)
    ext = np.zeros((n_heads, 1, LANES), np.float32)
    kpos = np.zeros((tk, LANES), np.float32)
    j = np.arange(tk)
    for i, c in enumerate(pieces):
        ext[:, 0, i] = float(LANES) * c
        ext[:, 0, ALIBI_SPLIT + i] = c
        kpos[:, i] = j // LANES
        kpos[:, ALIBI_SPLIT + i] = j % LANES
        kpos[:, 2 * ALIBI_SPLIT + i] = 1.0
    assert tk // LANES <= 256, "key offsets must stay exact in bf16"
    return jnp.asarray(slope2), jnp.asarray(ext), jnp.asarray(kpos, BF16)


def diff_attention(qk, vt, bounded, seq_lens, n_heads, lq1, lk1, lq2, lk2, g_sub, lambda_init,
                   *, tq=1024, tk=2048, kc=256):
    m = qk.shape[0]
    dv = 2 * LANES
    g_all = math.gcd(*seq_lens)
    tq, tk = _tile(g_all, tq, LANES), _tile(g_all, tk, LANES)
    tabs = _step_tables(seq_lens, tq, tk)
    n_steps = tabs[0].shape[0]
    sgn = tabs[-1].astype(F32)
    slope2, ext, kpos = _alibi_tables(n_heads, tk)
    vec = lambda a: a.reshape(1, -1)
    small = pl.BlockSpec((1, LANES), lambda h, s, *_: (0, 0))
    return pl.pallas_call(
        functools.partial(_diff_body, lambda_init=lambda_init, kc=_tile(tk, kc, LANES)),
        out_shape=jax.ShapeDtypeStruct((m, n_heads * dv), BF16),
        grid_spec=pltpu.PrefetchScalarGridSpec(
            num_scalar_prefetch=10,
            grid=(n_heads, n_steps),
            in_specs=[pl.BlockSpec((tq, dv), lambda h, s, qt, kt, *_: (qt[s], h)),
                      pl.BlockSpec((tk, dv), lambda h, s, qt, kt, *_: (kt[s], n_heads + h)),
                      pl.BlockSpec((dv, tk), lambda h, s, qt, kt, *_: (h, kt[s])),
                      pl.BlockSpec((tk, LANES), lambda h, s, *_: (0, 0)),
                      pl.BlockSpec((None, 1, LANES), lambda h, s, *_: (h, 0, 0)),
                      small, small, small, small,
                      pl.BlockSpec((1, dv), lambda h, s, *_: (0, 0))],
            out_specs=pl.BlockSpec((tq, dv), lambda h, s, qt, kt, *_: (qt[s], h)),
            scratch_shapes=[pltpu.VMEM((2, 1, tq), F32),
                            pltpu.VMEM((2, dv + ONES_ROWS, tq), F32)]),
        compiler_params=_params("parallel", "arbitrary"),
        name="diff_attention",
    )(*tabs, bounded, sgn, slope2, qk, qk, vt, kpos, ext, vec(lq1), vec(lk1), vec(lq2), vec(lk2), vec(g_sub))


HALO = 16


def _dwconv_body(hp_ref, hn_ref, prev_ref, cur_ref, next_ref, w_ref, b_ref, o_ref, win_ref, *, width):
    i = pl.program_id(0)
    tt = cur_ref.shape[0]
    pad = width // 2
    zeros = jnp.zeros(prev_ref.shape, F32)
    win_ref[0:HALO, :] = jnp.where(hp_ref[i] == 1, prev_ref[...].astype(F32), zeros)
    win_ref[HALO:HALO + tt, :] = cur_ref[...].astype(F32)
    win_ref[HALO + tt:, :] = jnp.where(hn_ref[i] == 1, next_ref[...].astype(F32), zeros)
    rc = 64
    first = HALO - pad
    span = rc + SUBLANES * ((width - 1) // SUBLANES) + SUBLANES
    for r0 in range(0, tt, rc):
        acc = jnp.zeros((rc, cur_ref.shape[1]), F32) + b_ref[...]
        block = win_ref[r0:r0 + span, :]
        for phase in range(SUBLANES):
            shift = (first + phase) % SUBLANES
            rolled = block if shift == 0 else pltpu.roll(block, span - shift, 0)
            for t in range(phase, width, SUBLANES):
                a = first + t - shift
                acc = acc + rolled[a:a + rc, :] * w_ref[t:t + 1, :]
        o_ref[r0:r0 + rc, :] = acc.astype(o_ref.dtype)


def dwconv(u, w_dw, b_dw, seq_lens, *, tt=256, tc=512):
    m, d = u.shape
    width = w_dw.shape[0]
    assert width // 2 <= HALO
    tt = _tile(math.gcd(*seq_lens), tt, 64)
    tc = _tile(d, tc, LANES)
    nb = tt // HALO
    starts = np.cumsum([0] + list(seq_lens))
    has_prev = np.ones(m // tt, np.int32)
    has_next = np.ones(m // tt, np.int32)
    for st in starts[:-1]:
        has_prev[st // tt] = 0
    for en in starts[1:]:
        has_next[en // tt - 1] = 0
    last_halo = m // HALO - 1
    return pl.pallas_call(
        functools.partial(_dwconv_body, width=width),
        out_shape=jax.ShapeDtypeStruct((m, d), F32),
        grid_spec=pltpu.PrefetchScalarGridSpec(
            num_scalar_prefetch=2,
            grid=(m // tt, d // tc),
            in_specs=[pl.BlockSpec((HALO, tc), lambda i, j, *_: (jnp.maximum(i * nb - 1, 0), j)),
                      pl.BlockSpec((tt, tc), lambda i, j, *_: (i, j)),
                      pl.BlockSpec((HALO, tc), lambda i, j, *_: (jnp.minimum((i + 1) * nb, last_halo), j)),
                      pl.BlockSpec((width, tc), lambda i, j, *_: (0, j)),
                      pl.BlockSpec((1, tc), lambda i, j, *_: (0, j))],
            out_specs=pl.BlockSpec((tt, tc), lambda i, j, *_: (i, j)),
            scratch_shapes=[pltpu.VMEM((tt + 2 * HALO, tc), F32)]),
        compiler_params=_params("parallel", "parallel"),
        name="dwconv",
    )(jnp.asarray(has_prev), jnp.asarray(has_next), u, u, u, w_dw, b_dw.reshape(1, d))


def _ln_silu_body(x_ref, g_ref, b_ref, o_ref):
    x = x_ref[...]
    mu = jnp.mean(x, axis=-1, keepdims=True)
    xc = x - mu
    var = jnp.mean(xc * xc, axis=-1, keepdims=True)
    y = xc * lax.rsqrt(var + EPS) * g_ref[...] + b_ref[...]
    o_ref[...] = jax.nn.silu(y).astype(o_ref.dtype)


def layernorm_silu(x, g, b):
    m, d = x.shape
    tm = _tile(m, 256, 8)
    return pl.pallas_call(
        _ln_silu_body,
        out_shape=jax.ShapeDtypeStruct((m, d), BF16),
        grid=(m // tm,),
        in_specs=[pl.BlockSpec((tm, d), lambda i: (i, 0)),
                  pl.BlockSpec((1, d), lambda i: (0, 0)),
                  pl.BlockSpec((1, d), lambda i: (0, 0))],
        out_specs=pl.BlockSpec((tm, d), lambda i: (i, 0)),
        compiler_params=_params("parallel"),
        name="layernorm_silu",
    )(x, g.reshape(1, d), b.reshape(1, d))


def kernel(x_prompt, x_sample, norm_mix, norm_ffn, a_w_qkv, a_w_o, a_q_norm, a_k_norm, b_w_pw1, b_b_pw1, b_w_dw, b_b_dw, b_ln_g, b_ln_b, b_w_pw2, b_b_pw2, c_w_qkv, c_w_o, c_q_norm, c_k_norm, c_lambda_q1, c_lambda_k1, c_lambda_q2, c_lambda_k2, c_subln, ffn_w1, ffn_w3, ffn_w2, moe_router, moe_w1, moe_w3, moe_w2):
    depth, d = norm_mix.shape
    bp, sp, _ = x_prompt.shape
    bs, ss, _ = x_sample.shape
    seq_lens = [sp] * bp + [ss] * bs
    mp = bp * sp
    x = jnp.concatenate([x_prompt.reshape(mp, d), x_sample.reshape(bs * ss, d)], axis=0)

    hd = a_q_norm.shape[-1]
    a_heads = d // hd
    a_kv = (a_w_qkv.shape[-1] - d) // (2 * hd)
    c_heads = d // (2 * c_q_norm.shape[-1])
    n_exp, _, f_exp = moe_w1.shape[1:]
    rope_tables = _rope_tables(seq_lens)
    bf = lambda w: w.astype(BF16)

    for i in range(depth):
        h = rmsnorm(x, norm_mix[i])
        mixer, j = i % N_MIXERS, i // N_MIXERS
        if mixer == 0:
            qkv = matmul(h, bf(a_w_qkv[j]))
            q_scale = hd ** -0.5 * LOG2E
            qk = qk_prep(qkv, d, a_kv * hd, a_q_norm[j], a_k_norm[j], q_scale, rope_tables)
            vt = transpose_cols(qkv, d + a_kv * hd, a_kv * hd)
            bounded = _score_bound(a_q_norm[j], a_k_norm[j], q_scale)
            o = gqa_attention(qk, vt, bounded, seq_lens, a_heads, a_kv)
            x = matmul_residual(o, bf(a_w_o[j]), x)
        elif mixer == 1:
            u = matmul_glu(h, bf(b_w_pw1[j]), b_b_pw1[j])
            u = dwconv(u, b_w_dw[j], b_b_dw[j], seq_lens)
            u = layernorm_silu(u, b_ln_g[j], b_ln_b[j])
            x = matmul_residual(u, bf(b_w_pw2[j]), x, b_b_pw2[j])
        else:
            lambda_init = 0.8 - 0.6 * math.exp(-0.3 * i)
            qkv = matmul(h, bf(c_w_qkv[j]))
            q_scale = c_q_norm.shape[-1] ** -0.5 * LOG2E
            qk = qk_prep(qkv, d, d, c_q_norm[j], c_k_norm[j], q_scale)
            vt = transpose_cols(qkv, 2 * d, d)
            bounded = _score_bound(c_q_norm[j], c_k_norm[j], q_scale)
            o = diff_attention(qk, vt, bounded, seq_lens, c_heads, c_lambda_q1[j], c_lambda_k1[j],
                               c_lambda_q2[j], c_lambda_k2[j], c_subln[j], lambda_init)
            x = matmul_residual(o, bf(c_w_o[j]), x)
        k = i // 2
        if i % 2 == 0:
            h = rmsnorm(x, norm_ffn[i])
            u = matmul_swiglu(h, bf(ffn_w1[k]), bf(ffn_w3[k]))
            x = matmul_residual(u, bf(ffn_w2[k]), x)
        else:
            h, gates = rmsnorm_router(x, norm_ffn[i], moe_router[k])
            u = matmul_swiglu_experts(h, bf(moe_w1[k]), bf(moe_w3[k]), gates)
            x = matmul_residual(u, bf(moe_w2[k]).reshape(n_exp * f_exp, d), x)

    return (x[:mp].reshape(bp, sp, d), x[mp:].reshape(bs, ss, d))
```

```python
import functools
import math

import jax
import jax.numpy as jnp
import numpy as np
from jax import lax
from jax.experimental import pallas as pl
from jax.experimental.pallas import tpu as pltpu

EPS = 1e-6
ROPE_THETA = 10000.0
GRID_W = 64
N_MIXERS = 3
LOG2E = 1.4426950408889634

LANES = 128
SUBLANES = 8
V7X_MXU_DIM = 256
V7X_VMEM_LIMIT_BYTES = 56 * 1024 * 1024

F32 = jnp.float32
BF16 = jnp.bfloat16


def _tile(dim, pref, align):
    if dim <= pref:
        return dim
    t = (pref // align) * align
    while t >= align:
        if dim % t == 0:
            return t
        t -= align
    return dim


def _params(*sem):
    return pltpu.CompilerParams(dimension_semantics=sem, vmem_limit_bytes=V7X_VMEM_LIMIT_BYTES)


class _RowParts:
    def __init__(self, rows, tm):
        self.tiles = [r // tm for r in rows]
        self.starts = [sum(self.tiles[:p]) for p in range(len(rows))]
        assert all(r % tm == 0 for r in rows)

    def active(self, p, i):
        return jnp.logical_and(i >= self.starts[p], i < self.starts[p] + self.tiles[p])

    def row_tile(self, p, i):
        return jnp.clip(i - self.starts[p], 0, self.tiles[p] - 1)

    def col_tile(self, p, i, j, nj):
        return jnp.where(i < self.starts[p], 0, jnp.where(i >= self.starts[p] + self.tiles[p], nj - 1, j))

    def select(self, i, refs):
        x = refs[-1][...]
        for p in range(len(refs) - 2, -1, -1):
            x = jnp.where(i < self.starts[p + 1], refs[p][...], x)
        return x


def _rmsnorm_body(*refs, parts):
    x_refs, (g_ref, o_ref) = refs[:-2], refs[-2:]
    x = parts.select(pl.program_id(0), x_refs)
    r = lax.rsqrt(jnp.mean(x * x, axis=-1, keepdims=True) + EPS)
    o_ref[...] = (x * r * g_ref[...]).astype(o_ref.dtype)


def rmsnorm(xs, g):
    d = xs[0].shape[1]
    rows = [x.shape[0] for x in xs]
    m = sum(rows)
    tm = _tile(math.gcd(*rows) if len(rows) > 1 else m, 256, 8)
    parts = _RowParts(rows, tm)
    return pl.pallas_call(
        functools.partial(_rmsnorm_body, parts=parts),
        out_shape=jax.ShapeDtypeStruct((m, d), BF16),
        grid=(m // tm,),
        in_specs=[pl.BlockSpec((tm, d), lambda i, p=p: (parts.row_tile(p, i), 0)) for p in range(len(xs))]
        + [pl.BlockSpec((1, d), lambda i: (0, 0))],
        out_specs=pl.BlockSpec((tm, d), lambda i: (i, 0)),
        compiler_params=_params("parallel"),
        name="rmsnorm",
    )(*xs, g.reshape(1, d))


def _rmsnorm_router_body(x_ref, g_ref, wr_ref, o_ref, gates_ref, *, n_experts):
    x = x_ref[...]
    r = lax.rsqrt(jnp.mean(x * x, axis=-1, keepdims=True) + EPS)
    h = x * r * g_ref[...]
    o_ref[...] = h.astype(o_ref.dtype)
    logits = jnp.dot(h, wr_ref[...], preferred_element_type=F32, precision=lax.Precision.HIGHEST)
    lane = lax.broadcasted_iota(jnp.int32, logits.shape, 1).astype(F32)
    neg = jnp.float32(-jnp.inf)
    lg = jnp.where(lane < n_experts, logits, neg)
    m1 = jnp.max(lg, axis=-1, keepdims=True)
    i1 = jnp.min(jnp.where(lg == m1, lane, float(LANES)), axis=-1, keepdims=True)
    lg2 = jnp.where(lane == i1, neg, lg)
    m2 = jnp.max(lg2, axis=-1, keepdims=True)
    i2 = jnp.min(jnp.where(lg2 == m2, lane, float(LANES)), axis=-1, keepdims=True)
    e2 = jnp.exp(m2 - m1)
    den = 1.0 + e2
    gates_ref[...] = jnp.where(lane == i1, 1.0 / den, 0.0) + jnp.where(lane == i2, e2 / den, 0.0)


def rmsnorm_router(x, g, w_router):
    m, d = x.shape
    e = w_router.shape[1]
    tm = _tile(m, 256, 8)
    wr = jnp.zeros((d, LANES), F32).at[:, :e].set(w_router)
    return pl.pallas_call(
        functools.partial(_rmsnorm_router_body, n_experts=e),
        out_shape=(jax.ShapeDtypeStruct((m, d), BF16), jax.ShapeDtypeStruct((m, LANES), F32)),
        grid=(m // tm,),
        in_specs=[pl.BlockSpec((tm, d), lambda i: (i, 0)),
                  pl.BlockSpec((1, d), lambda i: (0, 0)),
                  pl.BlockSpec((d, LANES), lambda i: (0, 0))],
        out_specs=(pl.BlockSpec((tm, d), lambda i: (i, 0)),
                   pl.BlockSpec((tm, LANES), lambda i: (i, 0))),
        compiler_params=_params("parallel"),
        name="rmsnorm_router",
    )(x, g.reshape(1, d), wr)


def _matmul_steps(nk, dots, acc_refs, finish):
    if nk == 1:
        finish(dots)
        return
    k = pl.program_id(2)

    @pl.when(k == 0)
    def _():
        for acc in acc_refs:
            acc[...] = jnp.zeros(acc.shape, F32)

    for acc, d in zip(acc_refs, dots):
        acc[...] += d

    @pl.when(k == nk - 1)
    def _():
        finish([acc[...] for acc in acc_refs])


def _mm_tiles(m, n, kd, tm, tn, tk=4096):
    tk = _tile(kd, tk, V7X_MXU_DIM)
    return _tile(m, tm, SUBLANES), _tile(n, tn, LANES), tk, kd // tk


def _acc_scratch(nk, count, tm, tn):
    return [pltpu.VMEM((tm, tn), F32)] * count if nk > 1 else []


def _head_rmsnorm(x, gain):
    return x * lax.rsqrt(jnp.mean(x * x, axis=-1, keepdims=True) + EPS) * gain


def _pipelined(n, produce, consume):
    item = produce(0)
    for c in range(n):
        ahead = produce(c + 1) if c + 1 < n else None
        consume(c, item)
        item = ahead


def _mm_qkv_body(x_ref, w_ref, gq_ref, gk_ref, o_ref, *acc, nk, n_q_blocks, n_qk_blocks, q_scale):
    j = pl.program_id(1)

    def finish(s):
        is_q = j < n_q_blocks
        is_v = j >= n_qk_blocks
        gain = jnp.where(is_q, gq_ref[...], gk_ref[...])
        post = jnp.where(is_q, jnp.float32(q_scale), jnp.float32(1.0))
        for h in range(o_ref.shape[1] // LANES):
            sl = slice(h * LANES, (h + 1) * LANES)
            x = s[0][:, sl]
            o_ref[:, sl] = jnp.where(is_v, x, _head_rmsnorm(x, gain) * post).astype(o_ref.dtype)

    _matmul_steps(nk, [jnp.dot(x_ref[...], w_ref[...], preferred_element_type=F32)], acc, finish)


def _mm_qkv_rope_body(x_ref, w_ref, gq_ref, gk_ref, cos_ref, sin_ref, perm_ref, o_ref,
                      *, n_q_blocks, n_qk_blocks, q_scale):
    j = pl.program_id(1)
    is_q = j < n_q_blocks
    is_v = j >= n_qk_blocks
    gain = jnp.where(is_q, gq_ref[...], gk_ref[...])
    post = jnp.where(is_q, jnp.float32(q_scale), jnp.float32(1.0))
    cos = jnp.concatenate([cos_ref[...]] * 2, axis=1)
    sin = jnp.concatenate([sin_ref[...]] * 2, axis=1)
    x = x_ref[...]
    cw = 2 * LANES
    cols = lambda c: slice(c * cw, (c + 1) * cw)

    def project(c):
        return jnp.dot(x, w_ref[:, cols(c)], preferred_element_type=F32)

    def finish(c, acc):
        y = jnp.concatenate([_head_rmsnorm(acc[:, :LANES], gain), _head_rmsnorm(acc[:, LANES:], gain)], axis=1)
        rot = jnp.dot(y.astype(perm_ref.dtype), perm_ref[...], preferred_element_type=F32)
        out = (y * cos + rot * sin) * post
        o_ref[:, cols(c)] = jnp.where(is_v, acc, out).astype(o_ref.dtype)

    _pipelined(o_ref.shape[1] // cw, project, finish)


def matmul_qkv(x, w, n_q_cols, n_k_cols, g_q, g_k, q_scale, rope_tables=None, *, tm=1024, tn=1024):
    m, kd = x.shape
    n = w.shape[1]
    hd = g_q.shape[0]
    assert hd == LANES
    n_v_cols = n - n_q_cols - n_k_cols
    tn = _tile(math.gcd(n_q_cols, n_k_cols, n_v_cols), tn, LANES)
    tm, tn, tk, nk = _mm_tiles(m, n, kd, tm, tn)
    in_specs = [pl.BlockSpec((tm, tk), lambda i, j, k: (i, k)),
                pl.BlockSpec((tk, tn), lambda i, j, k: (k, j)),
                pl.BlockSpec((1, hd), lambda i, j, k: (0, 0)),
                pl.BlockSpec((1, hd), lambda i, j, k: (0, 0))]
    args = [x, w, g_q.reshape(1, hd), g_k.reshape(1, hd)]
    blocks = dict(n_q_blocks=n_q_cols // tn, n_qk_blocks=(n_q_cols + n_k_cols) // tn, q_scale=q_scale)
    if rope_tables is None:
        body = functools.partial(_mm_qkv_body, nk=nk, **blocks)
    else:
        assert nk == 1 and tn % (2 * hd) == 0
        cos, sin, perm = rope_tables
        in_specs += [pl.BlockSpec((tm, hd), lambda i, j, k: (i, 0))] * 2
        in_specs.append(pl.BlockSpec((2 * hd, 2 * hd), lambda i, j, k: (0, 0)))
        args += [cos, sin, perm]
        body = functools.partial(_mm_qkv_rope_body, **blocks)
    return pl.pallas_call(
        body,
        out_shape=jax.ShapeDtypeStruct((m, n), BF16),
        grid=(m // tm, n // tn, nk),
        in_specs=in_specs,
        out_specs=pl.BlockSpec((tm, tn), lambda i, j, k: (i, j)),
        scratch_shapes=_acc_scratch(nk, 1, tm, tn),
        compiler_params=_params("parallel", "parallel", "arbitrary"),
        name="matmul_qkv",
    )(*args)


def _mm_resid_body(*refs, nk, has_bias, r_parts, o_parts):
    n_r, n_o = len(r_parts.tiles), len(o_parts.tiles)
    x_ref, w_ref = refs[:2]
    r_refs = refs[2:2 + n_r]
    b_ref = refs[2 + n_r] if has_bias else None
    o_refs = refs[2 + n_r + has_bias:2 + n_r + has_bias + n_o]
    acc = refs[2 + n_r + has_bias + n_o:]
    i = pl.program_id(0)

    def finish(s):
        out = s[0] + b_ref[...] if has_bias else s[0]
        out = r_parts.select(i, r_refs) + out
        if n_o == 1:
            o_refs[0][...] = out
        else:
            for p, o_ref in enumerate(o_refs):
                @pl.when(o_parts.active(p, i))
                def _(o_ref=o_ref):
                    o_ref[...] = out

    _matmul_steps(nk, [jnp.dot(x_ref[...], w_ref[...], preferred_element_type=F32)], acc, finish)


def matmul_residual(x, w, resid, bias=None, out_rows=None, *, tm=1024, tn=512):
    m, kd = x.shape
    n = w.shape[1]
    if kd > 4096:
        tm = tm // 2
    r_rows = [r.shape[0] for r in resid]
    o_rows = list(out_rows) if out_rows else [m]
    tm = min(tm, math.gcd(*r_rows, *o_rows))
    tm, tn, tk, nk = _mm_tiles(m, n, kd, tm, tn, tk=8192)
    nj = n // tn
    r_parts, o_parts = _RowParts(r_rows, tm), _RowParts(o_rows, tm)

    def part_spec(parts, p):
        return pl.BlockSpec((tm, tn), lambda i, j, k: (parts.row_tile(p, i), parts.col_tile(p, i, j, nj)))

    in_specs = [pl.BlockSpec((tm, tk), lambda i, j, k: (i, k)),
                pl.BlockSpec((tk, tn), lambda i, j, k: (k, j))]
    in_specs += [part_spec(r_parts, p) for p in range(len(resid))]
    args = [x, w, *resid]
    if bias is not None:
        in_specs.append(pl.BlockSpec((1, tn), lambda i, j, k: (0, j)))
        args.append(bias.reshape(1, n))
    out = pl.pallas_call(
        functools.partial(_mm_resid_body, nk=nk, has_bias=bias is not None, r_parts=r_parts, o_parts=o_parts),
        out_shape=[jax.ShapeDtypeStruct((r, n), F32) for r in o_rows],
        grid=(m // tm, nj, nk),
        in_specs=in_specs,
        out_specs=[part_spec(o_parts, p) for p in range(len(o_rows))],
        scratch_shapes=_acc_scratch(nk, 1, tm, tn),
        compiler_params=(_params("arbitrary", "arbitrary", "arbitrary") if out_rows
                         else _params("parallel", "parallel", "arbitrary")),
        name="matmul_residual",
    )(*args)
    return out if out_rows else out[0]


def _mm_glu_body(x_ref, wa_ref, wg_ref, ba_ref, bg_ref, o_ref, *acc, nk):
    def finish(s):
        a = s[0] + ba_ref[...]
        g = s[1] + bg_ref[...]
        o_ref[...] = (a * jax.nn.sigmoid(g)).astype(o_ref.dtype)

    x = x_ref[...]
    _matmul_steps(nk, [jnp.dot(x, wa_ref[...], preferred_element_type=F32),
                       jnp.dot(x, wg_ref[...], preferred_element_type=F32)], acc, finish)


def matmul_glu(x, w, b, *, tm=1024, tn=512):
    m, kd = x.shape
    n = w.shape[1] // 2
    tm, tn, tk, nk = _mm_tiles(m, n, kd, tm, tn)
    nj = n // tn
    b2 = b.reshape(1, 2 * n)
    return pl.pallas_call(
        functools.partial(_mm_glu_body, nk=nk),
        out_shape=jax.ShapeDtypeStruct((m, n), BF16),
        grid=(m // tm, nj, nk),
        in_specs=[pl.BlockSpec((tm, tk), lambda i, j, k: (i, k)),
                  pl.BlockSpec((tk, tn), lambda i, j, k: (k, j)),
                  pl.BlockSpec((tk, tn), lambda i, j, k: (k, j + nj)),
                  pl.BlockSpec((1, tn), lambda i, j, k: (0, j)),
                  pl.BlockSpec((1, tn), lambda i, j, k: (0, j + nj))],
        out_specs=pl.BlockSpec((tm, tn), lambda i, j, k: (i, j)),
        scratch_shapes=_acc_scratch(nk, 2, tm, tn),
        compiler_params=_params("parallel", "parallel", "arbitrary"),
        name="matmul_glu",
    )(x, w, w, b2, b2)


def _mm_swiglu_body(*refs, nk, gated):
    x_ref, w1_ref, w3_ref = refs[:3]
    g_ref = refs[3] if gated else None
    o_ref = refs[3 + gated]
    acc = refs[4 + gated:]

    def finish(s):
        u = jax.nn.silu(s[0]) * s[1]
        if gated:
            gates = g_ref[...]
            lane = lax.broadcasted_iota(jnp.int32, gates.shape, 1)
            gate = jnp.sum(jnp.where(lane == pl.program_id(1), gates, 0.0), axis=-1, keepdims=True)
            u = u * gate
        o_ref[...] = u.astype(o_ref.dtype)

    x = x_ref[...]
    _matmul_steps(nk, [jnp.dot(x, w1_ref[...], preferred_element_type=F32),
                       jnp.dot(x, w3_ref[...], preferred_element_type=F32)], acc, finish)


def matmul_swiglu(x, w1, w3, *, tm=1024, tn=512):
    m, kd = x.shape
    n = w1.shape[1]
    tm, tn, tk, nk = _mm_tiles(m, n, kd, tm, tn)
    return pl.pallas_call(
        functools.partial(_mm_swiglu_body, nk=nk, gated=False),
        out_shape=jax.ShapeDtypeStruct((m, n), BF16),
        grid=(m // tm, n // tn, nk),
        in_specs=[pl.BlockSpec((tm, tk), lambda i, j, k: (i, k)),
                  pl.BlockSpec((tk, tn), lambda i, j, k: (k, j)),
                  pl.BlockSpec((tk, tn), lambda i, j, k: (k, j))],
        out_specs=pl.BlockSpec((tm, tn), lambda i, j, k: (i, j)),
        scratch_shapes=_acc_scratch(nk, 2, tm, tn),
        compiler_params=_params("parallel", "parallel", "arbitrary"),
        name="matmul_swiglu",
    )(x, w1, w3)


def matmul_swiglu_experts(x, w1, w3, gates, *, tm=1024):
    m, kd = x.shape
    e, _, f = w1.shape
    tm, _, tk, nk = _mm_tiles(m, f, kd, tm, f)
    return pl.pallas_call(
        functools.partial(_mm_swiglu_body, nk=nk, gated=True),
        out_shape=jax.ShapeDtypeStruct((m, e * f), BF16),
        grid=(m // tm, e, nk),
        in_specs=[pl.BlockSpec((tm, tk), lambda i, j, k: (i, k)),
                  pl.BlockSpec((None, tk, f), lambda i, j, k: (j, k, 0)),
                  pl.BlockSpec((None, tk, f), lambda i, j, k: (j, k, 0)),
                  pl.BlockSpec((tm, LANES), lambda i, j, k: (i, 0))],
        out_specs=pl.BlockSpec((tm, f), lambda i, j, k: (i, j)),
        scratch_shapes=_acc_scratch(nk, 2, tm, f),
        compiler_params=_params("parallel", "parallel", "arbitrary"),
        name="matmul_swiglu_experts",
    )(x, w1, w3, gates)


def _rope_tables(seq_lens):
    half = LANES // 2
    inv_freq = ROPE_THETA ** (-jnp.arange(0, half, 2, dtype=F32) / half)
    cos_l, sin_l = [], []
    for s in seq_lens:
        t = jnp.arange(s, dtype=jnp.int32)
        row = (t // GRID_W).astype(F32)
        col = (t % GRID_W).astype(F32)
        ang_r = row[:, None] * inv_freq[None, :]
        ang_c = col[:, None] * inv_freq[None, :]
        emb = jnp.concatenate([ang_r, ang_r, ang_c, ang_c], axis=-1)
        cos_l.append(jnp.cos(emb))
        sin_l.append(jnp.sin(emb))
    cos = jnp.concatenate(cos_l, axis=0)
    sin = jnp.concatenate(sin_l, axis=0)
    quarter = half // 2
    perm = np.zeros((2 * LANES, 2 * LANES), np.float32)
    for i in range(2 * LANES):
        if i % half < quarter:
            perm[i + quarter, i] = -1.0
        else:
            perm[i - quarter, i] = 1.0
    return cos, sin, jnp.asarray(perm, BF16)


def _transpose_body(x_ref, o_ref):
    o_ref[...] = x_ref[...].astype(F32).T.astype(o_ref.dtype)


def transpose_cols(x, col0, ncols, *, tm=512, bw=512):
    m = x.shape[0]
    tm = _tile(m, tm, LANES)
    bw = _tile(math.gcd(ncols, col0) if col0 else ncols, bw, LANES)
    c0 = col0 // bw
    return pl.pallas_call(
        _transpose_body,
        out_shape=jax.ShapeDtypeStruct((ncols, m), x.dtype),
        grid=(m // tm, ncols // bw),
        in_specs=[pl.BlockSpec((tm, bw), lambda i, j: (i, c0 + j))],
        out_specs=pl.BlockSpec((bw, tm), lambda i, j: (j, i)),
        compiler_params=_params("parallel", "parallel"),
        name="transpose_cols",
    )(x)


def _step_tables(seq_lens, tq, tk):
    qt, kt, first, last, qrel, krel, side = [], [], [], [], [], [], []
    start = 0
    for s in seq_lens:
        for qi in range(s // tq):
            nkv = s // tk
            for ki in range(nkv):
                qt.append(start // tq + qi)
                kt.append(start // tk + ki)
                first.append(int(ki == 0))
                last.append(int(ki == nkv - 1))
                qrel.append(qi * tq)
                krel.append(ki * tk)
                side.append(1 if (ki + 1) * tk <= qi * tq else (-1 if ki * tk >= (qi + 1) * tq else 0))
        start += s
    return [jnp.asarray(np.asarray(a, np.int32)) for a in (qt, kt, first, last, qrel, krel, side)]


ONES_ROWS = 16


def _with_ones_rows(vt):
    return jnp.concatenate([vt, jnp.ones((ONES_ROWS, vt.shape[1]), vt.dtype)], axis=0)


def _online_softmax_step(st, vt1, m_ref, acc_ref, idx, shift=None):
    m_prev = m_ref[idx]
    tile_max = jnp.max(st, axis=0, keepdims=True)
    if shift is not None:
        tile_max = tile_max + shift
    m_new = jnp.maximum(m_prev, tile_max)
    alpha = jnp.exp2(m_prev - m_new)
    p = jnp.exp2(st - (m_new if shift is None else m_new - shift))
    acc_ref[idx] = alpha * acc_ref[idx] + jnp.dot(vt1, p.astype(vt1.dtype), preferred_element_type=F32)
    m_ref[idx] = m_new


SAFE_LOG2 = 60.0


def _score_bound(g_q, g_k, q_scale):
    bound = q_scale * LANES * jnp.max(jnp.abs(g_q)) * jnp.max(jnp.abs(g_k))
    return (bound <= SAFE_LOG2).astype(jnp.int32).reshape(1)


def _pv_fixed_reference(st, vt1):
    return jnp.dot(vt1, jnp.exp2(st).astype(vt1.dtype), preferred_element_type=F32)


def _scores_t(k, q):
    return lax.dot_general(k, q, (((1,), (1,)), ((), ())), preferred_element_type=F32)


def _init_softmax_state(m_ref, acc_ref):
    m_ref[...] = jnp.full(m_ref.shape, -jnp.inf, F32)
    acc_ref[...] = jnp.zeros(acc_ref.shape, F32)


def _gqa_body(qt_ref, kt_ref, first_ref, last_ref, qrel_ref, krel_ref, side_ref, bounded_ref,
              q_ref, k_ref, vt_ref, o_ref, m_ref, acc_ref, *, group, kc):
    s = pl.program_id(1)

    @pl.when(first_ref[s] == 1)
    def _():
        _init_softmax_state(m_ref, acc_ref)

    def scores(c):
        k = k_ref[c * kc:(c + 1) * kc, :]
        return [_scores_t(k, q_ref[:, g * LANES:(g + 1) * LANES]) for g in range(group)]

    @pl.when(bounded_ref[0] == 1)
    def _():
        pv = [[] for _ in range(group)]

        def softmax(c, sts):
            vt1 = _with_ones_rows(vt_ref[:, c * kc:(c + 1) * kc])
            for g in range(group):
                pv[g].append(_pv_fixed_reference(sts[g], vt1))

        _pipelined(k_ref.shape[0] // kc, scores, softmax)
        for g in range(group):
            acc_ref[g] += functools.reduce(lambda a, b: a + b, pv[g])

    @pl.when(bounded_ref[0] == 0)
    def _():
        def softmax(c, sts):
            vt1 = _with_ones_rows(vt_ref[:, c * kc:(c + 1) * kc])
            for g in range(group):
                _online_softmax_step(sts[g], vt1, m_ref, acc_ref, g)

        _pipelined(k_ref.shape[0] // kc, scores, softmax)

    @pl.when(last_ref[s] == 1)
    def _():
        for g in range(group):
            acc = acc_ref[g]
            o = acc[:LANES] / acc[LANES:LANES + 1]
            o_ref[:, g * LANES:(g + 1) * LANES] = o.T.astype(o_ref.dtype)


def gqa_attention(qk, vt, bounded, seq_lens, n_heads, n_kv, *, tq=1024, tk=2048, kc=256):
    m = qk.shape[0]
    group = n_heads // n_kv
    g_all = math.gcd(*seq_lens)
    tq, tk = _tile(g_all, tq, LANES), _tile(g_all, tk, LANES)
    tabs = _step_tables(seq_lens, tq, tk)
    n_steps = tabs[0].shape[0]
    return pl.pallas_call(
        functools.partial(_gqa_body, group=group, kc=_tile(tk, kc, LANES)),
        out_shape=jax.ShapeDtypeStruct((m, n_heads * LANES), BF16),
        grid_spec=pltpu.PrefetchScalarGridSpec(
            num_scalar_prefetch=8,
            grid=(n_kv, n_steps),
            in_specs=[pl.BlockSpec((tq, group * LANES), lambda h, s, qt, kt, *_: (qt[s], h)),
                      pl.BlockSpec((tk, LANES), lambda h, s, qt, kt, *_: (kt[s], n_heads + h)),
                      pl.BlockSpec((LANES, tk), lambda h, s, qt, kt, *_: (h, kt[s]))],
            out_specs=pl.BlockSpec((tq, group * LANES), lambda h, s, qt, kt, *_: (qt[s], h)),
            scratch_shapes=[pltpu.VMEM((group, 1, tq), F32),
                            pltpu.VMEM((group, LANES + ONES_ROWS, tq), F32)]),
        compiler_params=_params("parallel", "arbitrary"),
        name="gqa_attention",
    )(*tabs, bounded, qk, qk, vt)


ALIBI_SPLIT = 3


def _diff_body(qt_ref, kt_ref, first_ref, last_ref, qrel_ref, krel_ref, side_ref, bounded_ref, sgn_ref, slope_ref,
               q_ref, k_ref, vt_ref, kpos_ref, ext_ref, lq1_ref, lk1_ref, lq2_ref, lk2_ref, gsub_ref,
               o_ref, m_ref, acc_ref, *, lambda_init, kc):
    h = pl.program_id(0)
    s = pl.program_id(1)

    @pl.when(first_ref[s] == 1)
    def _():
        _init_softmax_state(m_ref, acc_ref)

    tq, tk = q_ref.shape[0], k_ref.shape[0]
    dv = vt_ref.shape[0]
    slope2 = slope_ref[h]
    subs = [slice(c * LANES, (c + 1) * LANES) for c in range(2)]
    rows = lambda c: slice(c * kc, (c + 1) * kc)

    def online(scores, shift=None):
        def softmax(c, sts):
            vt1 = _with_ones_rows(vt_ref[:, rows(c)])
            for sub in range(2):
                _online_softmax_step(sts[sub], vt1, m_ref, acc_ref, sub, shift)

        _pipelined(tk // kc, scores, softmax)

    def fixed_reference(scores):
        pv = [[], []]

        def softmax(c, sts):
            vt1 = _with_ones_rows(vt_ref[:, rows(c)])
            for sub in range(2):
                pv[sub].append(_pv_fixed_reference(sts[sub], vt1))

        _pipelined(tk // kc, scores, softmax)
        for sub in range(2):
            acc_ref[sub] += functools.reduce(lambda a, b: a + b, pv[sub])

    bounded = bounded_ref[0] == 1

    @pl.when(side_ref[s] == 0)
    def _():
        keys = lax.broadcasted_iota(jnp.int32, (kc, tq), 0)
        queries = lax.broadcasted_iota(jnp.int32, (kc, tq), 1)
        delta = keys - queries + (krel_ref[s] - qrel_ref[s])

        def scores(c):
            bias = jnp.abs(delta + c * kc).astype(F32) * (-slope2)
            return [_scores_t(k_ref[rows(c), sl], q_ref[:, sl]) + bias for sl in subs]

        pl.when(bounded)(lambda: fixed_reference(scores))
        pl.when(jnp.logical_not(bounded))(lambda: online(scores))

    @pl.when(side_ref[s] != 0)
    def _():
        sgn = sgn_ref[s]
        ext_row = ext_ref[...] * sgn

        def augmented(q_ext):
            q_aug = [jnp.concatenate([q_ref[:, sl], q_ext.astype(q_ref.dtype)], axis=1) for sl in subs]

            def scores(c):
                k_ext = kpos_ref[rows(c), :]
                return [_scores_t(jnp.concatenate([k_ref[rows(c), sl], k_ext], axis=1), q_aug[sub])
                        for sub, sl in enumerate(subs)]

            return scores

        @pl.when(bounded)
        def _():
            qpos = qrel_ref[s] + lax.broadcasted_iota(jnp.int32, (tq, LANES), 0)
            rest = (sgn * slope2) * (krel_ref[s] - qpos).astype(F32)
            lane = lax.broadcasted_iota(jnp.int32, (tq, LANES), 1)
            q_ext = jnp.broadcast_to(ext_row, (tq, LANES))
            for i in range(ALIBI_SPLIT):
                piece = rest.astype(BF16).astype(F32)
                q_ext = jnp.where(lane == 2 * ALIBI_SPLIT + i, piece, q_ext)
                rest = rest - piece
            fixed_reference(augmented(q_ext))

        @pl.when(jnp.logical_not(bounded))
        def _():
            qpos = qrel_ref[s] + lax.broadcasted_iota(jnp.int32, (1, tq), 1)
            shift = (sgn * slope2) * (krel_ref[s] - qpos).astype(F32)
            online(augmented(jnp.broadcast_to(ext_row, (tq, LANES))), shift)

    @pl.when(last_ref[s] == 1)
    def _():
        lam = (jnp.exp(jnp.sum(lq1_ref[...] * lk1_ref[...], axis=-1, keepdims=True))
               - jnp.exp(jnp.sum(lq2_ref[...] * lk2_ref[...], axis=-1, keepdims=True)) + lambda_init)
        acc0, acc1 = acc_ref[0], acc_ref[1]
        o = (acc0[:dv] * (1.0 / acc0[dv:dv + 1])
             - acc1[:dv] * (lam * (1.0 / acc1[dv:dv + 1])))
        r = lax.rsqrt(jnp.mean(o * o, axis=0, keepdims=True) + EPS)
        o_ref[...] = (((o * r).T * gsub_ref[...]) * (1.0 - lambda_init)).astype(o_ref.dtype)


def _alibi_tables(n_heads, tk):
    slope2 = (2.0 ** (-8.0 * np.arange(1, n_heads + 1, dtype=np.float64) / n_heads) * LOG2E).astype(np.float32)
    to_bf16 = lambda a: a.astype(BF16).astype(np.float32)
    pieces, rest = [], slope2.copy()
    for _ in range(ALIBI_SPLIT):
        c = to_bf16(rest)
        pieces.append(c)
        rest = (rest - c).astype(np.float32)
    ext = np.zeros((n_heads, 1, LANES), np.float32)
    kpos = np.zeros((tk, LANES), np.float32)
    j = np.arange(tk)
    for i, c in enumerate(pieces):
        ext[:, 0, i] = float(LANES) * c
        ext[:, 0, ALIBI_SPLIT + i] = c
        kpos[:, i] = j // LANES
        kpos[:, ALIBI_SPLIT + i] = j % LANES
        kpos[:, 2 * ALIBI_SPLIT + i] = 1.0
    assert tk // LANES <= 256, "key offsets must stay exact in bf16"
    return jnp.asarray(slope2), jnp.asarray(ext), jnp.asarray(kpos, BF16)


def diff_attention(qk, vt, bounded, seq_lens, n_heads, lq1, lk1, lq2, lk2, g_sub, lambda_init,
                   *, tq=1024, tk=2048, kc=256):
    m = qk.shape[0]
    dv = 2 * LANES
    g_all = math.gcd(*seq_lens)
    tq, tk = _tile(g_all, tq, LANES), _tile(g_all, tk, LANES)
    tabs = _step_tables(seq_lens, tq, tk)
    n_steps = tabs[0].shape[0]
    sgn = tabs[-1].astype(F32)
    slope2, ext, kpos = _alibi_tables(n_heads, tk)
    vec = lambda a: a.reshape(1, -1)
    small = pl.BlockSpec((1, LANES), lambda h, s, *_: (0, 0))
    return pl.pallas_call(
        functools.partial(_diff_body, lambda_init=lambda_init, kc=_tile(tk, kc, LANES)),
        out_shape=jax.ShapeDtypeStruct((m, n_heads * dv), BF16),
        grid_spec=pltpu.PrefetchScalarGridSpec(
            num_scalar_prefetch=10,
            grid=(n_heads, n_steps),
            in_specs=[pl.BlockSpec((tq, dv), lambda h, s, qt, kt, *_: (qt[s], h)),
                      pl.BlockSpec((tk, dv), lambda h, s, qt, kt, *_: (kt[s], n_heads + h)),
                      pl.BlockSpec((dv, tk), lambda h, s, qt, kt, *_: (h, kt[s])),
                      pl.BlockSpec((tk, LANES), lambda h, s, *_: (0, 0)),
                      pl.BlockSpec((None, 1, LANES), lambda h, s, *_: (h, 0, 0)),
                      small, small, small, small,
                      pl.BlockSpec((1, dv), lambda h, s, *_: (0, 0))],
            out_specs=pl.BlockSpec((tq, dv), lambda h, s, qt, kt, *_: (qt[s], h)),
            scratch_shapes=[pltpu.VMEM((2, 1, tq), F32),
                            pltpu.VMEM((2, dv + ONES_ROWS, tq), F32)]),
        compiler_params=_params("parallel", "arbitrary"),
        name="diff_attention",
    )(*tabs, bounded, sgn, slope2, qk, qk, vt, kpos, ext, vec(lq1), vec(lk1), vec(lq2), vec(lk2), vec(g_sub))


HALO = 16


def _dwconv_body(hp_ref, hn_ref, prev_ref, cur_ref, next_ref, w_ref, b_ref, o_ref, win_ref, *, width):
    i = pl.program_id(0)
    tt = cur_ref.shape[0]
    pad = width // 2
    zeros = jnp.zeros(prev_ref.shape, F32)
    win_ref[0:HALO, :] = jnp.where(hp_ref[i] == 1, prev_ref[...].astype(F32), zeros)
    win_ref[HALO:HALO + tt, :] = cur_ref[...].astype(F32)
    win_ref[HALO + tt:, :] = jnp.where(hn_ref[i] == 1, next_ref[...].astype(F32), zeros)
    rc = 64
    first = HALO - pad
    span = rc + SUBLANES * ((width - 1) // SUBLANES) + SUBLANES
    for r0 in range(0, tt, rc):
        acc = jnp.zeros((rc, cur_ref.shape[1]), F32) + b_ref[...]
        block = win_ref[r0:r0 + span, :]
        for phase in range(SUBLANES):
            shift = (first + phase) % SUBLANES
            rolled = block if shift == 0 else pltpu.roll(block, span - shift, 0)
            for t in range(phase, width, SUBLANES):
                a = first + t - shift
                acc = acc + rolled[a:a + rc, :] * w_ref[t:t + 1, :]
        o_ref[r0:r0 + rc, :] = acc.astype(o_ref.dtype)


def dwconv(u, w_dw, b_dw, seq_lens, *, tt=256, tc=512):
    m, d = u.shape
    width = w_dw.shape[0]
    assert width // 2 <= HALO
    tt = _tile(math.gcd(*seq_lens), tt, 64)
    tc = _tile(d, tc, LANES)
    nb = tt // HALO
    starts = np.cumsum([0] + list(seq_lens))
    has_prev = np.ones(m // tt, np.int32)
    has_next = np.ones(m // tt, np.int32)
    for st in starts[:-1]:
        has_prev[st // tt] = 0
    for en in starts[1:]:
        has_next[en // tt - 1] = 0
    last_halo = m // HALO - 1
    return pl.pallas_call(
        functools.partial(_dwconv_body, width=width),
        out_shape=jax.ShapeDtypeStruct((m, d), F32),
        grid_spec=pltpu.PrefetchScalarGridSpec(
            num_scalar_prefetch=2,
            grid=(m // tt, d // tc),
            in_specs=[pl.BlockSpec((HALO, tc), lambda i, j, *_: (jnp.maximum(i * nb - 1, 0), j)),
                      pl.BlockSpec((tt, tc), lambda i, j, *_: (i, j)),
                      pl.BlockSpec((HALO, tc), lambda i, j, *_: (jnp.minimum((i + 1) * nb, last_halo), j)),
                      pl.BlockSpec((width, tc), lambda i, j, *_: (0, j)),
                      pl.BlockSpec((1, tc), lambda i, j, *_: (0, j))],
            out_specs=pl.BlockSpec((tt, tc), lambda i, j, *_: (i, j)),
            scratch_shapes=[pltpu.VMEM((tt + 2 * HALO, tc), F32)]),
        compiler_params=_params("parallel", "parallel"),
        name="dwconv",
    )(jnp.asarray(has_prev), jnp.asarray(has_next), u, u, u, w_dw, b_dw.reshape(1, d))


def _ln_silu_body(x_ref, g_ref, b_ref, o_ref):
    x = x_ref[...]
    mu = jnp.mean(x, axis=-1, keepdims=True)
    xc = x - mu
    var = jnp.mean(xc * xc, axis=-1, keepdims=True)
    y = xc * lax.rsqrt(var + EPS) * g_ref[...] + b_ref[...]
    o_ref[...] = jax.nn.silu(y).astype(o_ref.dtype)


def layernorm_silu(x, g, b):
    m, d = x.shape
    tm = _tile(m, 256, 8)
    return pl.pallas_call(
        _ln_silu_body,
        out_shape=jax.ShapeDtypeStruct((m, d), BF16),
        grid=(m // tm,),
        in_specs=[pl.BlockSpec((tm, d), lambda i: (i, 0)),
                  pl.BlockSpec((1, d), lambda i: (0, 0)),
                  pl.BlockSpec((1, d), lambda i: (0, 0))],
        out_specs=pl.BlockSpec((tm, d), lambda i: (i, 0)),
        compiler_params=_params("parallel"),
        name="layernorm_silu",
    )(x, g.reshape(1, d), b.reshape(1, d))


def kernel(x_prompt, x_sample, norm_mix, norm_ffn, a_w_qkv, a_w_o, a_q_norm, a_k_norm, b_w_pw1, b_b_pw1, b_w_dw, b_b_dw, b_ln_g, b_ln_b, b_w_pw2, b_b_pw2, c_w_qkv, c_w_o, c_q_norm, c_k_norm, c_lambda_q1, c_lambda_k1, c_lambda_q2, c_lambda_k2, c_subln, ffn_w1, ffn_w3, ffn_w2, moe_router, moe_w1, moe_w3, moe_w2):
    depth, d = norm_mix.shape
    bp, sp, _ = x_prompt.shape
    bs, ss, _ = x_sample.shape
    seq_lens = [sp] * bp + [ss] * bs
    mp, ms = bp * sp, bs * ss
    x = [x_prompt.reshape(mp, d), x_sample.reshape(ms, d)]

    hd = a_q_norm.shape[-1]
    a_heads = d // hd
    a_kv = (a_w_qkv.shape[-1] - d) // (2 * hd)
    c_heads = d // (2 * c_q_norm.shape[-1])
    n_exp, _, f_exp = moe_w1.shape[1:]
    rope_tables = _rope_tables(seq_lens)
    bf = lambda w: w.astype(BF16)

    for i in range(depth):
        h = rmsnorm(x, norm_mix[i])
        mixer, j = i % N_MIXERS, i // N_MIXERS
        if mixer == 0:
            q_scale = hd ** -0.5 * LOG2E
            qkv = matmul_qkv(h, bf(a_w_qkv[j]), d, a_kv * hd, a_q_norm[j], a_k_norm[j], q_scale, rope_tables)
            vt = transpose_cols(qkv, d + a_kv * hd, a_kv * hd)
            bounded = _score_bound(a_q_norm[j], a_k_norm[j], q_scale)
            o = gqa_attention(qkv, vt, bounded, seq_lens, a_heads, a_kv)
            x = [matmul_residual(o, bf(a_w_o[j]), x)]
        elif mixer == 1:
            u = matmul_glu(h, bf(b_w_pw1[j]), b_b_pw1[j])
            u = dwconv(u, b_w_dw[j], b_b_dw[j], seq_lens)
            u = layernorm_silu(u, b_ln_g[j], b_ln_b[j])
            x = [matmul_residual(u, bf(b_w_pw2[j]), x, b_b_pw2[j])]
        else:
            lambda_init = 0.8 - 0.6 * math.exp(-0.3 * i)
            q_scale = c_q_norm.shape[-1] ** -0.5 * LOG2E
            qkv = matmul_qkv(h, bf(c_w_qkv[j]), d, d, c_q_norm[j], c_k_norm[j], q_scale)
            vt = transpose_cols(qkv, 2 * d, d)
            bounded = _score_bound(c_q_norm[j], c_k_norm[j], q_scale)
            o = diff_attention(qkv, vt, bounded, seq_lens, c_heads, c_lambda_q1[j], c_lambda_k1[j],
                               c_lambda_q2[j], c_lambda_k2[j], c_subln[j], lambda_init)
            x = [matmul_residual(o, bf(c_w_o[j]), x)]
        k = i // 2
        out_rows = [mp, ms] if i == depth - 1 else None
        if i % 2 == 0:
            h = rmsnorm(x, norm_ffn[i])
            u = matmul_swiglu(h, bf(ffn_w1[k]), bf(ffn_w3[k]))
            x = matmul_residual(u, bf(ffn_w2[k]), x, out_rows=out_rows)
        else:
            h, gates = rmsnorm_router(x[0], norm_ffn[i], moe_router[k])
            u = matmul_swiglu_experts(h, bf(moe_w1[k]), bf(moe_w3[k]), gates)
            x = matmul_residual(u, bf(moe_w2[k]).reshape(n_exp * f_exp, d), x, out_rows=out_rows)
        x = x if out_rows else [x]

    return (x[0].reshape(bp, sp, d), x[1].reshape(bs, ss, d))
```

```python
import functools
import math

import jax
import jax.numpy as jnp
import numpy as np
from jax import lax
from jax.experimental import pallas as pl
from jax.experimental.pallas import tpu as pltpu

EPS = 1e-6
ROPE_THETA = 10000.0
GRID_W = 64
N_MIXERS = 3
LOG2E = 1.4426950408889634

LANES = 128
SUBLANES = 8
V7X_MXU_DIM = 256
V7X_VMEM_LIMIT_BYTES = 56 * 1024 * 1024

F32 = jnp.float32
BF16 = jnp.bfloat16


def _tile(dim, pref, align):
    if dim <= pref:
        return dim
    t = (pref // align) * align
    while t >= align:
        if dim % t == 0:
            return t
        t -= align
    return dim


def _params(*sem):
    return pltpu.CompilerParams(dimension_semantics=sem, vmem_limit_bytes=V7X_VMEM_LIMIT_BYTES)


class _RowParts:
    def __init__(self, rows, tm):
        self.tiles = [r // tm for r in rows]
        self.starts = [sum(self.tiles[:p]) for p in range(len(rows))]
        assert all(r % tm == 0 for r in rows)

    def active(self, p, i):
        return jnp.logical_and(i >= self.starts[p], i < self.starts[p] + self.tiles[p])

    def row_tile(self, p, i):
        return jnp.clip(i - self.starts[p], 0, self.tiles[p] - 1)

    def col_tile(self, p, i, j, nj):
        return jnp.where(i < self.starts[p], 0, jnp.where(i >= self.starts[p] + self.tiles[p], nj - 1, j))

    def select(self, i, refs):
        x = refs[-1][...]
        for p in range(len(refs) - 2, -1, -1):
            x = jnp.where(i < self.starts[p + 1], refs[p][...], x)
        return x


def _rmsnorm_body(*refs, parts):
    x_refs, (g_ref, o_ref) = refs[:-2], refs[-2:]
    x = parts.select(pl.program_id(0), x_refs)
    r = lax.rsqrt(jnp.mean(x * x, axis=-1, keepdims=True) + EPS)
    o_ref[...] = (x * r * g_ref[...]).astype(o_ref.dtype)


def rmsnorm(xs, g):
    d = xs[0].shape[1]
    rows = [x.shape[0] for x in xs]
    m = sum(rows)
    tm = _tile(math.gcd(*rows) if len(rows) > 1 else m, 256, 8)
    parts = _RowParts(rows, tm)
    return pl.pallas_call(
        functools.partial(_rmsnorm_body, parts=parts),
        out_shape=jax.ShapeDtypeStruct((m, d), BF16),
        grid=(m // tm,),
        in_specs=[pl.BlockSpec((tm, d), lambda i, p=p: (parts.row_tile(p, i), 0)) for p in range(len(xs))]
        + [pl.BlockSpec((1, d), lambda i: (0, 0))],
        out_specs=pl.BlockSpec((tm, d), lambda i: (i, 0)),
        compiler_params=_params("parallel"),
        name="rmsnorm",
    )(*xs, g.reshape(1, d))


def _rmsnorm_router_body(x_ref, g_ref, wr_ref, o_ref, gates_ref, *, n_experts):
    x = x_ref[...]
    r = lax.rsqrt(jnp.mean(x * x, axis=-1, keepdims=True) + EPS)
    h = x * r * g_ref[...]
    o_ref[...] = h.astype(o_ref.dtype)
    logits = jnp.dot(h, wr_ref[...], preferred_element_type=F32, precision=lax.Precision.HIGHEST)
    lane = lax.broadcasted_iota(jnp.int32, logits.shape, 1).astype(F32)
    neg = jnp.float32(-jnp.inf)
    lg = jnp.where(lane < n_experts, logits, neg)
    m1 = jnp.max(lg, axis=-1, keepdims=True)
    i1 = jnp.min(jnp.where(lg == m1, lane, float(LANES)), axis=-1, keepdims=True)
    lg2 = jnp.where(lane == i1, neg, lg)
    m2 = jnp.max(lg2, axis=-1, keepdims=True)
    i2 = jnp.min(jnp.where(lg2 == m2, lane, float(LANES)), axis=-1, keepdims=True)
    e2 = jnp.exp(m2 - m1)
    den = 1.0 + e2
    gates_ref[...] = jnp.where(lane == i1, 1.0 / den, 0.0) + jnp.where(lane == i2, e2 / den, 0.0)


def rmsnorm_router(x, g, w_router):
    m, d = x.shape
    e = w_router.shape[1]
    tm = _tile(m, 256, 8)
    wr = jnp.zeros((d, LANES), F32).at[:, :e].set(w_router)
    return pl.pallas_call(
        functools.partial(_rmsnorm_router_body, n_experts=e),
        out_shape=(jax.ShapeDtypeStruct((m, d), BF16), jax.ShapeDtypeStruct((m, LANES), F32)),
        grid=(m // tm,),
        in_specs=[pl.BlockSpec((tm, d), lambda i: (i, 0)),
                  pl.BlockSpec((1, d), lambda i: (0, 0)),
                  pl.BlockSpec((d, LANES), lambda i: (0, 0))],
        out_specs=(pl.BlockSpec((tm, d), lambda i: (i, 0)),
                   pl.BlockSpec((tm, LANES), lambda i: (i, 0))),
        compiler_params=_params("parallel"),
        name="rmsnorm_router",
    )(x, g.reshape(1, d), wr)


def _matmul_steps(nk, dots, acc_refs, finish):
    if nk == 1:
        finish(dots)
        return
    k = pl.program_id(2)

    @pl.when(k == 0)
    def _():
        for acc in acc_refs:
            acc[...] = jnp.zeros(acc.shape, F32)

    for acc, d in zip(acc_refs, dots):
        acc[...] += d

    @pl.when(k == nk - 1)
    def _():
        finish([acc[...] for acc in acc_refs])


def _mm_tiles(m, n, kd, tm, tn, tk=4096):
    tk = _tile(kd, tk, V7X_MXU_DIM)
    return _tile(m, tm, SUBLANES), _tile(n, tn, LANES), tk, kd // tk


def _acc_scratch(nk, count, tm, tn):
    return [pltpu.VMEM((tm, tn), F32)] * count if nk > 1 else []


def _head_rmsnorm(x, gain):
    return x * lax.rsqrt(jnp.mean(x * x, axis=-1, keepdims=True) + EPS) * gain


def _pipelined(n, produce, consume):
    item = produce(0)
    for c in range(n):
        ahead = produce(c + 1) if c + 1 < n else None
        consume(c, item)
        item = ahead


def _mm_qkv_body(x_ref, w_ref, gq_ref, gk_ref, o_ref, *acc, nk, n_q_blocks, n_qk_blocks, q_scale):
    j = pl.program_id(1)

    def finish(s):
        is_q = j < n_q_blocks
        is_v = j >= n_qk_blocks
        gain = jnp.where(is_q, gq_ref[...], gk_ref[...])
        post = jnp.where(is_q, jnp.float32(q_scale), jnp.float32(1.0))
        for h in range(o_ref.shape[1] // LANES):
            sl = slice(h * LANES, (h + 1) * LANES)
            x = s[0][:, sl]
            o_ref[:, sl] = jnp.where(is_v, x, _head_rmsnorm(x, gain) * post).astype(o_ref.dtype)

    _matmul_steps(nk, [jnp.dot(x_ref[...], w_ref[...], preferred_element_type=F32)], acc, finish)


def _mm_qkv_rope_body(x_ref, w_ref, gq_ref, gk_ref, cos_ref, sin_ref, perm_ref, o_ref,
                      *, n_q_blocks, n_qk_blocks, q_scale):
    j = pl.program_id(1)
    is_q = j < n_q_blocks
    is_v = j >= n_qk_blocks
    gain = jnp.where(is_q, gq_ref[...], gk_ref[...])
    post = jnp.where(is_q, jnp.float32(q_scale), jnp.float32(1.0))
    cos = jnp.concatenate([cos_ref[...]] * 2, axis=1)
    sin = jnp.concatenate([sin_ref[...]] * 2, axis=1)
    x = x_ref[...]
    cw = 2 * LANES
    cols = lambda c: slice(c * cw, (c + 1) * cw)

    def project(c):
        return jnp.dot(x, w_ref[:, cols(c)], preferred_element_type=F32)

    def finish(c, acc):
        y = jnp.concatenate([_head_rmsnorm(acc[:, :LANES], gain), _head_rmsnorm(acc[:, LANES:], gain)], axis=1)
        rot = jnp.dot(y.astype(perm_ref.dtype), perm_ref[...], preferred_element_type=F32)
        out = (y * cos + rot * sin) * post
        o_ref[:, cols(c)] = jnp.where(is_v, acc, out).astype(o_ref.dtype)

    _pipelined(o_ref.shape[1] // cw, project, finish)


def matmul_qkv(x, w, n_q_cols, n_k_cols, g_q, g_k, q_scale, rope_tables=None, *, tm=1024, tn=1024):
    m, kd = x.shape
    n = w.shape[1]
    hd = g_q.shape[0]
    assert hd == LANES
    n_v_cols = n - n_q_cols - n_k_cols
    tn = _tile(math.gcd(n_q_cols, n_k_cols, n_v_cols), tn, LANES)
    tm, tn, tk, nk = _mm_tiles(m, n, kd, tm, tn)
    in_specs = [pl.BlockSpec((tm, tk), lambda i, j, k: (i, k)),
                pl.BlockSpec((tk, tn), lambda i, j, k: (k, j)),
                pl.BlockSpec((1, hd), lambda i, j, k: (0, 0)),
                pl.BlockSpec((1, hd), lambda i, j, k: (0, 0))]
    args = [x, w, g_q.reshape(1, hd), g_k.reshape(1, hd)]
    blocks = dict(n_q_blocks=n_q_cols // tn, n_qk_blocks=(n_q_cols + n_k_cols) // tn, q_scale=q_scale)
    if rope_tables is None:
        body = functools.partial(_mm_qkv_body, nk=nk, **blocks)
    else:
        assert nk == 1 and tn % (2 * hd) == 0
        cos, sin, perm = rope_tables
        in_specs += [pl.BlockSpec((tm, hd), lambda i, j, k: (i, 0))] * 2
        in_specs.append(pl.BlockSpec((2 * hd, 2 * hd), lambda i, j, k: (0, 0)))
        args += [cos, sin, perm]
        body = functools.partial(_mm_qkv_rope_body, **blocks)
    return pl.pallas_call(
        body,
        out_shape=jax.ShapeDtypeStruct((m, n), BF16),
        grid=(m // tm, n // tn, nk),
        in_specs=in_specs,
        out_specs=pl.BlockSpec((tm, tn), lambda i, j, k: (i, j)),
        scratch_shapes=_acc_scratch(nk, 1, tm, tn),
        compiler_params=_params("parallel", "parallel", "arbitrary"),
        name="matmul_qkv",
    )(*args)


def _mm_resid_body(*refs, nk, has_bias, r_parts, o_parts):
    n_r, n_o = len(r_parts.tiles), len(o_parts.tiles)
    x_ref, w_ref = refs[:2]
    r_refs = refs[2:2 + n_r]
    b_ref = refs[2 + n_r] if has_bias else None
    o_refs = refs[2 + n_r + has_bias:2 + n_r + has_bias + n_o]
    acc = refs[2 + n_r + has_bias + n_o:]
    i = pl.program_id(0)

    def finish(s):
        out = s[0] + b_ref[...] if has_bias else s[0]
        out = r_parts.select(i, r_refs) + out
        if n_o == 1:
            o_refs[0][...] = out
        else:
            for p, o_ref in enumerate(o_refs):
                @pl.when(o_parts.active(p, i))
                def _(o_ref=o_ref):
                    o_ref[...] = out

    _matmul_steps(nk, [jnp.dot(x_ref[...], w_ref[...], preferred_element_type=F32)], acc, finish)


def matmul_residual(x, w, resid, bias=None, out_rows=None, *, tm=1024, tn=512):
    m, kd = x.shape
    n = w.shape[1]
    if kd > 4096:
        tm = tm // 2
    r_rows = [r.shape[0] for r in resid]
    o_rows = list(out_rows) if out_rows else [m]
    tm = min(tm, math.gcd(*r_rows, *o_rows))
    tm, tn, tk, nk = _mm_tiles(m, n, kd, tm, tn, tk=8192)
    nj = n // tn
    r_parts, o_parts = _RowParts(r_rows, tm), _RowParts(o_rows, tm)

    def part_spec(parts, p):
        return pl.BlockSpec((tm, tn), lambda i, j, k: (parts.row_tile(p, i), parts.col_tile(p, i, j, nj)))

    in_specs = [pl.BlockSpec((tm, tk), lambda i, j, k: (i, k)),
                pl.BlockSpec((tk, tn), lambda i, j, k: (k, j))]
    in_specs += [part_spec(r_parts, p) for p in range(len(resid))]
    args = [x, w, *resid]
    if bias is not None:
        in_specs.append(pl.BlockSpec((1, tn), lambda i, j, k: (0, j)))
        args.append(bias.reshape(1, n))
    out = pl.pallas_call(
        functools.partial(_mm_resid_body, nk=nk, has_bias=bias is not None, r_parts=r_parts, o_parts=o_parts),
        out_shape=[jax.ShapeDtypeStruct((r, n), F32) for r in o_rows],
        grid=(m // tm, nj, nk),
        in_specs=in_specs,
        out_specs=[part_spec(o_parts, p) for p in range(len(o_rows))],
        scratch_shapes=_acc_scratch(nk, 1, tm, tn),
        compiler_params=(_params("arbitrary", "arbitrary", "arbitrary") if out_rows
                         else _params("parallel", "parallel", "arbitrary")),
        name="matmul_residual",
    )(*args)
    return out if out_rows else out[0]


def _mm_glu_body(x_ref, wa_ref, wg_ref, ba_ref, bg_ref, o_ref, *acc, nk):
    def finish(s):
        a = s[0] + ba_ref[...]
        g = s[1] + bg_ref[...]
        o_ref[...] = (a * jax.nn.sigmoid(g)).astype(o_ref.dtype)

    x = x_ref[...]
    _matmul_steps(nk, [jnp.dot(x, wa_ref[...], preferred_element_type=F32),
                       jnp.dot(x, wg_ref[...], preferred_element_type=F32)], acc, finish)


def matmul_glu(x, w, b, *, tm=1024, tn=512):
    m, kd = x.shape
    n = w.shape[1] // 2
    tm, tn, tk, nk = _mm_tiles(m, n, kd, tm, tn)
    nj = n // tn
    b2 = b.reshape(1, 2 * n)
    return pl.pallas_call(
        functools.partial(_mm_glu_body, nk=nk),
        out_shape=jax.ShapeDtypeStruct((m, n), BF16),
        grid=(m // tm, nj, nk),
        in_specs=[pl.BlockSpec((tm, tk), lambda i, j, k: (i, k)),
                  pl.BlockSpec((tk, tn), lambda i, j, k: (k, j)),
                  pl.BlockSpec((tk, tn), lambda i, j, k: (k, j + nj)),
                  pl.BlockSpec((1, tn), lambda i, j, k: (0, j)),
                  pl.BlockSpec((1, tn), lambda i, j, k: (0, j + nj))],
        out_specs=pl.BlockSpec((tm, tn), lambda i, j, k: (i, j)),
        scratch_shapes=_acc_scratch(nk, 2, tm, tn),
        compiler_params=_params("parallel", "parallel", "arbitrary"),
        name="matmul_glu",
    )(x, w, w, b2, b2)


def _mm_swiglu_body(*refs, nk, gated):
    x_ref, w1_ref, w3_ref = refs[:3]
    g_ref = refs[3] if gated else None
    o_ref = refs[3 + gated]
    acc = refs[4 + gated:]

    def finish(s):
        u = jax.nn.silu(s[0]) * s[1]
        if gated:
            gates = g_ref[...]
            lane = lax.broadcasted_iota(jnp.int32, gates.shape, 1)
            gate = jnp.sum(jnp.where(lane == pl.program_id(1), gates, 0.0), axis=-1, keepdims=True)
            u = u * gate
        o_ref[...] = u.astype(o_ref.dtype)

    x = x_ref[...]
    _matmul_steps(nk, [jnp.dot(x, w1_ref[...], preferred_element_type=F32),
                       jnp.dot(x, w3_ref[...], preferred_element_type=F32)], acc, finish)


def matmul_swiglu(x, w1, w3, *, tm=1024, tn=512):
    m, kd = x.shape
    n = w1.shape[1]
    tm, tn, tk, nk = _mm_tiles(m, n, kd, tm, tn)
    return pl.pallas_call(
        functools.partial(_mm_swiglu_body, nk=nk, gated=False),
        out_shape=jax.ShapeDtypeStruct((m, n), BF16),
        grid=(m // tm, n // tn, nk),
        in_specs=[pl.BlockSpec((tm, tk), lambda i, j, k: (i, k)),
                  pl.BlockSpec((tk, tn), lambda i, j, k: (k, j)),
                  pl.BlockSpec((tk, tn), lambda i, j, k: (k, j))],
        out_specs=pl.BlockSpec((tm, tn), lambda i, j, k: (i, j)),
        scratch_shapes=_acc_scratch(nk, 2, tm, tn),
        compiler_params=_params("parallel", "parallel", "arbitrary"),
        name="matmul_swiglu",
    )(x, w1, w3)


def matmul_swiglu_experts(x, w1, w3, gates, *, tm=1024):
    m, kd = x.shape
    e, _, f = w1.shape
    tm, _, tk, nk = _mm_tiles(m, f, kd, tm, f)
    return pl.pallas_call(
        functools.partial(_mm_swiglu_body, nk=nk, gated=True),
        out_shape=jax.ShapeDtypeStruct((m, e * f), BF16),
        grid=(m // tm, e, nk),
        in_specs=[pl.BlockSpec((tm, tk), lambda i, j, k: (i, k)),
                  pl.BlockSpec((None, tk, f), lambda i, j, k: (j, k, 0)),
                  pl.BlockSpec((None, tk, f), lambda i, j, k: (j, k, 0)),
                  pl.BlockSpec((tm, LANES), lambda i, j, k: (i, 0))],
        out_specs=pl.BlockSpec((tm, f), lambda i, j, k: (i, j)),
        scratch_shapes=_acc_scratch(nk, 2, tm, f),
        compiler_params=_params("parallel", "parallel", "arbitrary"),
        name="matmul_swiglu_experts",
    )(x, w1, w3, gates)


def _rope_tables(seq_lens):
    half = LANES // 2
    inv_freq = ROPE_THETA ** (-jnp.arange(0, half, 2, dtype=F32) / half)
    cos_l, sin_l = [], []
    for s in seq_lens:
        t = jnp.arange(s, dtype=jnp.int32)
        row = (t // GRID_W).astype(F32)
        col = (t % GRID_W).astype(F32)
        ang_r = row[:, None] * inv_freq[None, :]
        ang_c = col[:, None] * inv_freq[None, :]
        emb = jnp.concatenate([ang_r, ang_r, ang_c, ang_c], axis=-1)
        cos_l.append(jnp.cos(emb))
        sin_l.append(jnp.sin(emb))
    cos = jnp.concatenate(cos_l, axis=0)
    sin = jnp.concatenate(sin_l, axis=0)
    quarter = half // 2
    perm = np.zeros((2 * LANES, 2 * LANES), np.float32)
    for i in range(2 * LANES):
        if i % half < quarter:
            perm[i + quarter, i] = -1.0
        else:
            perm[i - quarter, i] = 1.0
    return cos, sin, jnp.asarray(perm, BF16)


def _transpose_body(x_ref, o_ref):
    o_ref[...] = x_ref[...].astype(F32).T.astype(o_ref.dtype)


def transpose_cols(x, col0, ncols, *, tm=512, bw=512):
    m = x.shape[0]
    tm = _tile(m, tm, LANES)
    bw = _tile(math.gcd(ncols, col0) if col0 else ncols, bw, LANES)
    c0 = col0 // bw
    return pl.pallas_call(
        _transpose_body,
        out_shape=jax.ShapeDtypeStruct((ncols, m), x.dtype),
        grid=(m // tm, ncols // bw),
        in_specs=[pl.BlockSpec((tm, bw), lambda i, j: (i, c0 + j))],
        out_specs=pl.BlockSpec((bw, tm), lambda i, j: (j, i)),
        compiler_params=_params("parallel", "parallel"),
        name="transpose_cols",
    )(x)


def _step_tables(seq_lens, tq, tk):
    qt, kt, first, last, qrel, krel, side = [], [], [], [], [], [], []
    start = 0
    for s in seq_lens:
        for qi in range(s // tq):
            nkv = s // tk
            for ki in range(nkv):
                qt.append(start // tq + qi)
                kt.append(start // tk + ki)
                first.append(int(ki == 0))
                last.append(int(ki == nkv - 1))
                qrel.append(qi * tq)
                krel.append(ki * tk)
                side.append(1 if (ki + 1) * tk <= qi * tq else (-1 if ki * tk >= (qi + 1) * tq else 0))
        start += s
    return [jnp.asarray(np.asarray(a, np.int32)) for a in (qt, kt, first, last, qrel, krel, side)]


ONES_ROWS = 16


def _with_ones_rows(vt):
    return jnp.concatenate([vt, jnp.ones((ONES_ROWS, vt.shape[1]), vt.dtype)], axis=0)


def _online_softmax_step(st, vt1, m_ref, acc_ref, idx, shift=None):
    m_prev = m_ref[idx]
    tile_max = jnp.max(st, axis=0, keepdims=True)
    if shift is not None:
        tile_max = tile_max + shift
    m_new = jnp.maximum(m_prev, tile_max)
    alpha = jnp.exp2(m_prev - m_new)
    p = jnp.exp2(st - (m_new if shift is None else m_new - shift))
    acc_ref[idx] = alpha * acc_ref[idx] + jnp.dot(vt1, p.astype(vt1.dtype), preferred_element_type=F32)
    m_ref[idx] = m_new


SAFE_LOG2 = 60.0


def _score_bound(g_q, g_k, q_scale):
    bound = q_scale * LANES * jnp.max(jnp.abs(g_q)) * jnp.max(jnp.abs(g_k))
    return (bound <= SAFE_LOG2).astype(jnp.int32).reshape(1)


def _pv_fixed_reference(st, vt1):
    return jnp.dot(vt1, jnp.exp2(st).astype(vt1.dtype), preferred_element_type=F32)


def _scores_t(k, q):
    return lax.dot_general(k, q, (((1,), (1,)), ((), ())), preferred_element_type=F32)


def _init_softmax_state(m_ref, acc_ref):
    m_ref[...] = jnp.full(m_ref.shape, -jnp.inf, F32)
    acc_ref[...] = jnp.zeros(acc_ref.shape, F32)


def _gqa_body(qt_ref, kt_ref, first_ref, last_ref, qrel_ref, krel_ref, side_ref, bounded_ref,
              q_ref, k_ref, vt_ref, o_ref, m_ref, acc_ref, *, group, kc):
    s = pl.program_id(1)

    @pl.when(first_ref[s] == 1)
    def _():
        _init_softmax_state(m_ref, acc_ref)

    def scores(c):
        k = k_ref[c * kc:(c + 1) * kc, :]
        return [_scores_t(k, q_ref[:, g * LANES:(g + 1) * LANES]) for g in range(group)]

    @pl.when(bounded_ref[0] == 1)
    def _():
        pv = [[] for _ in range(group)]

        def softmax(c, sts):
            vt1 = _with_ones_rows(vt_ref[:, c * kc:(c + 1) * kc])
            for g in range(group):
                pv[g].append(_pv_fixed_reference(sts[g], vt1))

        _pipelined(k_ref.shape[0] // kc, scores, softmax)
        for g in range(group):
            acc_ref[g] += functools.reduce(lambda a, b: a + b, pv[g])

    @pl.when(bounded_ref[0] == 0)
    def _():
        def softmax(c, sts):
            vt1 = _with_ones_rows(vt_ref[:, c * kc:(c + 1) * kc])
            for g in range(group):
                _online_softmax_step(sts[g], vt1, m_ref, acc_ref, g)

        _pipelined(k_ref.shape[0] // kc, scores, softmax)

    @pl.when(last_ref[s] == 1)
    def _():
        for g in range(group):
            acc = acc_ref[g]
            o = acc[:LANES] / acc[LANES:LANES + 1]
            o_ref[:, g * LANES:(g + 1) * LANES] = o.T.astype(o_ref.dtype)


def gqa_attention(qk, vt, bounded, seq_lens, n_heads, n_kv, *, tq=1024, tk=2048, kc=256):
    m = qk.shape[0]
    group = n_heads // n_kv
    g_all = math.gcd(*seq_lens)
    tq, tk = _tile(g_all, tq, LANES), _tile(g_all, tk, LANES)
    tabs = _step_tables(seq_lens, tq, tk)
    n_steps = tabs[0].shape[0]
    return pl.pallas_call(
        functools.partial(_gqa_body, group=group, kc=_tile(tk, kc, LANES)),
        out_shape=jax.ShapeDtypeStruct((m, n_heads * LANES), BF16),
        grid_spec=pltpu.PrefetchScalarGridSpec(
            num_scalar_prefetch=8,
            grid=(n_kv, n_steps),
            in_specs=[pl.BlockSpec((tq, group * LANES), lambda h, s, qt, kt, *_: (qt[s], h)),
                      pl.BlockSpec((tk, LANES), lambda h, s, qt, kt, *_: (kt[s], n_heads + h)),
                      pl.BlockSpec((LANES, tk), lambda h, s, qt, kt, *_: (h, kt[s]))],
            out_specs=pl.BlockSpec((tq, group * LANES), lambda h, s, qt, kt, *_: (qt[s], h)),
            scratch_shapes=[pltpu.VMEM((group, 1, tq), F32),
                            pltpu.VMEM((group, LANES + ONES_ROWS, tq), F32)]),
        compiler_params=_params("parallel", "arbitrary"),
        name="gqa_attention",
    )(*tabs, bounded, qk, qk, vt)


ALIBI_SPLIT = 3


def _diff_body(qt_ref, kt_ref, first_ref, last_ref, qrel_ref, krel_ref, side_ref, bounded_ref, sgn_ref, slope_ref,
               q_ref, k_ref, vt_ref, kpos_ref, ext_ref, lq1_ref, lk1_ref, lq2_ref, lk2_ref, gsub_ref,
               o_ref, m_ref, acc_ref, *, lambda_init, kc):
    h = pl.program_id(0)
    s = pl.program_id(1)

    @pl.when(first_ref[s] == 1)
    def _():
        _init_softmax_state(m_ref, acc_ref)

    tq, tk = q_ref.shape[0], k_ref.shape[0]
    dv = vt_ref.shape[0]
    slope2 = slope_ref[h]
    subs = [slice(c * LANES, (c + 1) * LANES) for c in range(2)]
    rows = lambda c: slice(c * kc, (c + 1) * kc)

    def online(scores, shift=None):
        def softmax(c, sts):
            vt1 = _with_ones_rows(vt_ref[:, rows(c)])
            for sub in range(2):
                _online_softmax_step(sts[sub], vt1, m_ref, acc_ref, sub, shift)

        _pipelined(tk // kc, scores, softmax)

    def fixed_reference(scores):
        pv = [[], []]

        def softmax(c, sts):
            vt1 = _with_ones_rows(vt_ref[:, rows(c)])
            for sub in range(2):
                pv[sub].append(_pv_fixed_reference(sts[sub], vt1))

        _pipelined(tk // kc, scores, softmax)
        for sub in range(2):
            acc_ref[sub] += functools.reduce(lambda a, b: a + b, pv[sub])

    bounded = bounded_ref[0] == 1

    @pl.when(side_ref[s] == 0)
    def _():
        keys = lax.broadcasted_iota(jnp.int32, (kc, tq), 0)
        queries = lax.broadcasted_iota(jnp.int32, (kc, tq), 1)
        delta = keys - queries + (krel_ref[s] - qrel_ref[s])

        def scores(c):
            bias = jnp.abs(delta + c * kc).astype(F32) * (-slope2)
            return [_scores_t(k_ref[rows(c), sl], q_ref[:, sl]) + bias for sl in subs]

        pl.when(bounded)(lambda: fixed_reference(scores))
        pl.when(jnp.logical_not(bounded))(lambda: online(scores))

    @pl.when(side_ref[s] != 0)
    def _():
        sgn = sgn_ref[s]
        ext_row = ext_ref[...] * sgn

        def augmented(q_ext):
            q_aug = [jnp.concatenate([q_ref[:, sl], q_ext.astype(q_ref.dtype)], axis=1) for sl in subs]

            def scores(c):
                k_ext = kpos_ref[rows(c), :]
                return [_scores_t(jnp.concatenate([k_ref[rows(c), sl], k_ext], axis=1), q_aug[sub])
                        for sub, sl in enumerate(subs)]

            return scores

        @pl.when(bounded)
        def _():
            qpos = qrel_ref[s] + lax.broadcasted_iota(jnp.int32, (tq, LANES), 0)
            rest = (sgn * slope2) * (krel_ref[s] - qpos).astype(F32)
            lane = lax.broadcasted_iota(jnp.int32, (tq, LANES), 1)
            q_ext = jnp.broadcast_to(ext_row, (tq, LANES))
            for i in range(ALIBI_SPLIT):
                piece = rest.astype(BF16).astype(F32)
                q_ext = jnp.where(lane == 2 * ALIBI_SPLIT + i, piece, q_ext)
                rest = rest - piece
            fixed_reference(augmented(q_ext))

        @pl.when(jnp.logical_not(bounded))
        def _():
            qpos = qrel_ref[s] + lax.broadcasted_iota(jnp.int32, (1, tq), 1)
            shift = (sgn * slope2) * (krel_ref[s] - qpos).astype(F32)
            online(augmented(jnp.broadcast_to(ext_row, (tq, LANES))), shift)

    @pl.when(last_ref[s] == 1)
    def _():
        lam = (jnp.exp(jnp.sum(lq1_ref[...] * lk1_ref[...], axis=-1, keepdims=True))
               - jnp.exp(jnp.sum(lq2_ref[...] * lk2_ref[...], axis=-1, keepdims=True)) + lambda_init)
        acc0, acc1 = acc_ref[0], acc_ref[1]
        o = (acc0[:dv] * (1.0 / acc0[dv:dv + 1])
             - acc1[:dv] * (lam * (1.0 / acc1[dv:dv + 1])))
        r = lax.rsqrt(jnp.mean(o * o, axis=0, keepdims=True) + EPS)
        o_ref[...] = (((o * r).T * gsub_ref[...]) * (1.0 - lambda_init)).astype(o_ref.dtype)


def _alibi_tables(n_heads, tk):
    slope2 = (2.0 ** (-8.0 * np.arange(1, n_heads + 1, dtype=np.float64) / n_heads) * LOG2E).astype(np.float32)
    to_bf16 = lambda a: a.astype(BF16).astype(np.float32)
    pieces, rest = [], slope2.copy()
    for _ in range(ALIBI_SPLIT):
        c = to_bf16(rest)
        pieces.append(c)
        rest = (rest - c).astype(np.float32)
    ext = np.zeros((n_heads, 1, LANES), np.float32)
    kpos = np.zeros((tk, LANES), np.float32)
    j = np.arange(tk)
    for i, c in enumerate(pieces):
        ext[:, 0, i] = float(LANES) * c
        ext[:, 0, ALIBI_SPLIT + i] = c
        kpos[:, i] = j // LANES
        kpos[:, ALIBI_SPLIT + i] = j % LANES
        kpos[:, 2 * ALIBI_SPLIT + i] = 1.0
    assert tk // LANES <= 256, "key offsets must stay exact in bf16"
    return jnp.asarray(slope2), jnp.asarray(ext), jnp.asarray(kpos, BF16)


def diff_attention(qk, vt, bounded, seq_lens, n_heads, lq1, lk1, lq2, lk2, g_sub, lambda_init,
                   *, tq=1024, tk=2048, kc=256):
    m = qk.shape[0]
    dv = 2 * LANES
    g_all = math.gcd(*seq_lens)
    tq, tk = _tile(g_all, tq, LANES), _tile(g_all, tk, LANES)
    tabs = _step_tables(seq_lens, tq, tk)
    n_steps = tabs[0].shape[0]
    sgn = tabs[-1].astype(F32)
    slope2, ext, kpos = _alibi_tables(n_heads, tk)
    vec = lambda a: a.reshape(1, -1)
    small = pl.BlockSpec((1, LANES), lambda h, s, *_: (0, 0))
    return pl.pallas_call(
        functools.partial(_diff_body, lambda_init=lambda_init, kc=_tile(tk, kc, LANES)),
        out_shape=jax.ShapeDtypeStruct((m, n_heads * dv), BF16),
        grid_spec=pltpu.PrefetchScalarGridSpec(
            num_scalar_prefetch=10,
            grid=(n_heads, n_steps),
            in_specs=[pl.BlockSpec((tq, dv), lambda h, s, qt, kt, *_: (qt[s], h)),
                      pl.BlockSpec((tk, dv), lambda h, s, qt, kt, *_: (kt[s], n_heads + h)),
                      pl.BlockSpec((dv, tk), lambda h, s, qt, kt, *_: (h, kt[s])),
                      pl.BlockSpec((tk, LANES), lambda h, s, *_: (0, 0)),
                      pl.BlockSpec((None, 1, LANES), lambda h, s, *_: (h, 0, 0)),
                      small, small, small, small,
                      pl.BlockSpec((1, dv), lambda h, s, *_: (0, 0))],
            out_specs=pl.BlockSpec((tq, dv), lambda h, s, qt, kt, *_: (qt[s], h)),
            scratch_shapes=[pltpu.VMEM((2, 1, tq), F32),
                            pltpu.VMEM((2, dv + ONES_ROWS, tq), F32)]),
        compiler_params=_params("parallel", "arbitrary"),
        name="diff_attention",
    )(*tabs, bounded, sgn, slope2, qk, qk, vt, kpos, ext, vec(lq1), vec(lk1), vec(lq2), vec(lk2), vec(g_sub))


HALO = 16


def _dwconv_ln_silu_body(hp_ref, hn_ref, prev_ref, cur_ref, next_ref, w_ref, b_ref, g_ref, beta_ref,
                         o_ref, win_ref, conv_ref, *, width, lc):
    i = pl.program_id(0)
    tt, d = cur_ref.shape
    pad = width // 2
    zeros = jnp.zeros(prev_ref.shape, F32)
    win_ref[0:HALO, :] = jnp.where(hp_ref[i] == 1, prev_ref[...].astype(F32), zeros)
    win_ref[HALO:HALO + tt, :] = cur_ref[...].astype(F32)
    win_ref[HALO + tt:, :] = jnp.where(hn_ref[i] == 1, next_ref[...].astype(F32), zeros)
    rc = 64
    first = HALO - pad
    span = rc + SUBLANES * ((width - 1) // SUBLANES) + SUBLANES

    def lane_chunk(c, carry):
        lanes = pl.ds(pl.multiple_of(c * lc, lc), lc)
        for r0 in range(0, tt, rc):
            acc = jnp.zeros((rc, lc), F32) + b_ref[:, lanes]
            block = win_ref[r0:r0 + span, lanes]
            for phase in range(SUBLANES):
                shift = (first + phase) % SUBLANES
                rolled = block if shift == 0 else pltpu.roll(block, span - shift, 0)
                for t in range(phase, width, SUBLANES):
                    a = first + t - shift
                    acc = acc + rolled[a:a + rc, :] * w_ref[t:t + 1, lanes]
            conv_ref[r0:r0 + rc, lanes] = acc
        return carry

    lax.fori_loop(0, d // lc, lane_chunk, 0)

    x = conv_ref[...]
    mu = jnp.mean(x, axis=-1, keepdims=True)
    xc = x - mu
    var = jnp.mean(xc * xc, axis=-1, keepdims=True)
    y = xc * lax.rsqrt(var + EPS) * g_ref[...] + beta_ref[...]
    o_ref[...] = jax.nn.silu(y).astype(o_ref.dtype)


def dwconv_ln_silu(u, w_dw, b_dw, ln_g, ln_b, seq_lens, *, tt=256, lc=512):
    m, d = u.shape
    width = w_dw.shape[0]
    assert width // 2 <= HALO
    tt = _tile(math.gcd(*seq_lens), tt, 64)
    lc = _tile(d, lc, LANES)
    nb = tt // HALO
    starts = np.cumsum([0] + list(seq_lens))
    has_prev = np.ones(m // tt, np.int32)
    has_next = np.ones(m // tt, np.int32)
    for st in starts[:-1]:
        has_prev[st // tt] = 0
    for en in starts[1:]:
        has_next[en // tt - 1] = 0
    last_halo = m // HALO - 1
    row = lambda a: a.reshape(1, d)
    vec_spec = pl.BlockSpec((1, d), lambda i, *_: (0, 0))
    return pl.pallas_call(
        functools.partial(_dwconv_ln_silu_body, width=width, lc=lc),
        out_shape=jax.ShapeDtypeStruct((m, d), BF16),
        grid_spec=pltpu.PrefetchScalarGridSpec(
            num_scalar_prefetch=2,
            grid=(m // tt,),
            in_specs=[pl.BlockSpec((HALO, d), lambda i, *_: (jnp.maximum(i * nb - 1, 0), 0)),
                      pl.BlockSpec((tt, d), lambda i, *_: (i, 0)),
                      pl.BlockSpec((HALO, d), lambda i, *_: (jnp.minimum((i + 1) * nb, last_halo), 0)),
                      pl.BlockSpec((width, d), lambda i, *_: (0, 0)),
                      vec_spec, vec_spec, vec_spec],
            out_specs=pl.BlockSpec((tt, d), lambda i, *_: (i, 0)),
            scratch_shapes=[pltpu.VMEM((tt + 2 * HALO, d), F32), pltpu.VMEM((tt, d), F32)]),
        compiler_params=_params("parallel"),
        name="dwconv_ln_silu",
    )(jnp.asarray(has_prev), jnp.asarray(has_next), u, u, u, w_dw, row(b_dw), row(ln_g), row(ln_b))


def _cast_body(w_ref, o_ref):
    o_ref[...] = w_ref[...].astype(o_ref.dtype)


def cast_layer(w, idx, *, block_bytes=4 * 1024 * 1024):
    lead, tail = w.shape[0], w.shape[1:]
    cols = tail[-1]
    rows = math.prod(tail[:-1])
    tc = _tile(cols, 2048, LANES)
    tr = _tile(rows, max(16, block_bytes // (4 * tc)), 16)
    out = pl.pallas_call(
        _cast_body,
        out_shape=jax.ShapeDtypeStruct((rows, cols), BF16),
        grid=(rows // tr, cols // tc),
        in_specs=[pl.BlockSpec((None, tr, tc), lambda i, j: (idx, i, j))],
        out_specs=pl.BlockSpec((tr, tc), lambda i, j: (i, j)),
        compiler_params=_params("parallel", "parallel"),
        name="cast_layer",
    )(w.reshape(lead, rows, cols))
    return out.reshape(tail)


def kernel(x_prompt, x_sample, norm_mix, norm_ffn, a_w_qkv, a_w_o, a_q_norm, a_k_norm, b_w_pw1, b_b_pw1, b_w_dw, b_b_dw, b_ln_g, b_ln_b, b_w_pw2, b_b_pw2, c_w_qkv, c_w_o, c_q_norm, c_k_norm, c_lambda_q1, c_lambda_k1, c_lambda_q2, c_lambda_k2, c_subln, ffn_w1, ffn_w3, ffn_w2, moe_router, moe_w1, moe_w3, moe_w2):
    depth, d = norm_mix.shape
    bp, sp, _ = x_prompt.shape
    bs, ss, _ = x_sample.shape
    seq_lens = [sp] * bp + [ss] * bs
    mp, ms = bp * sp, bs * ss
    x = [x_prompt.reshape(mp, d), x_sample.reshape(ms, d)]

    hd = a_q_norm.shape[-1]
    a_heads = d // hd
    a_kv = (a_w_qkv.shape[-1] - d) // (2 * hd)
    c_heads = d // (2 * c_q_norm.shape[-1])
    n_exp, _, f_exp = moe_w1.shape[1:]
    rope_tables = _rope_tables(seq_lens)
    bf = cast_layer

    for i in range(depth):
        h = rmsnorm(x, norm_mix[i])
        mixer, j = i % N_MIXERS, i // N_MIXERS
        if mixer == 0:
            q_scale = hd ** -0.5 * LOG2E
            qkv = matmul_qkv(h, bf(a_w_qkv, j), d, a_kv * hd, a_q_norm[j], a_k_norm[j], q_scale, rope_tables)
            vt = transpose_cols(qkv, d + a_kv * hd, a_kv * hd)
            bounded = _score_bound(a_q_norm[j], a_k_norm[j], q_scale)
            o = gqa_attention(qkv, vt, bounded, seq_lens, a_heads, a_kv)
            x = [matmul_residual(o, bf(a_w_o, j), x)]
        elif mixer == 1:
            u = matmul_glu(h, bf(b_w_pw1, j), b_b_pw1[j])
            u = dwconv_ln_silu(u, b_w_dw[j], b_b_dw[j], b_ln_g[j], b_ln_b[j], seq_lens)
            x = [matmul_residual(u, bf(b_w_pw2, j), x, b_b_pw2[j])]
        else:
            lambda_init = 0.8 - 0.6 * math.exp(-0.3 * i)
            q_scale = c_q_norm.shape[-1] ** -0.5 * LOG2E
            qkv = matmul_qkv(h, bf(c_w_qkv, j), d, d, c_q_norm[j], c_k_norm[j], q_scale)
            vt = transpose_cols(qkv, 2 * d, d)
            bounded = _score_bound(c_q_norm[j], c_k_norm[j], q_scale)
            o = diff_attention(qkv, vt, bounded, seq_lens, c_heads, c_lambda_q1[j], c_lambda_k1[j],
                               c_lambda_q2[j], c_lambda_k2[j], c_subln[j], lambda_init)
            x = [matmul_residual(o, bf(c_w_o, j), x)]
        k = i // 2
        out_rows = [mp, ms] if i == depth - 1 else None
        if i % 2 == 0:
            h = rmsnorm(x, norm_ffn[i])
            u = matmul_swiglu(h, bf(ffn_w1, k), bf(ffn_w3, k))
            x = matmul_residual(u, bf(ffn_w2, k), x, out_rows=out_rows)
        else:
            h, gates = rmsnorm_router(x[0], norm_ffn[i], moe_router[k])
            u = matmul_swiglu_experts(h, bf(moe_w1, k), bf(moe_w3, k), gates)
            x = matmul_residual(u, bf(moe_w2, k).reshape(n_exp * f_exp, d), x, out_rows=out_rows)
        x = x if out_rows else [x]

    return (x[0].reshape(bp, sp, d), x[1].reshape(bs, ss, d))
```

```python
import functools
import math

import jax
import jax.numpy as jnp
import numpy as np
from jax import lax
from jax.experimental import pallas as pl
from jax.experimental.pallas import tpu as pltpu

EPS = 1e-6
ROPE_THETA = 10000.0
GRID_W = 64
N_MIXERS = 3
LOG2E = 1.4426950408889634

LANES = 128
SUBLANES = 8
V7X_MXU_DIM = 256
V7X_VMEM_LIMIT_BYTES = 56 * 1024 * 1024

F32 = jnp.float32
BF16 = jnp.bfloat16


def _tile(dim, pref, align):
    if dim <= pref:
        return dim
    t = (pref // align) * align
    while t >= align:
        if dim % t == 0:
            return t
        t -= align
    return dim


def _params(*sem):
    return pltpu.CompilerParams(dimension_semantics=sem, vmem_limit_bytes=V7X_VMEM_LIMIT_BYTES)


class _RowParts:
    def __init__(self, rows, tm):
        self.tiles = [r // tm for r in rows]
        self.starts = [sum(self.tiles[:p]) for p in range(len(rows))]
        assert all(r % tm == 0 for r in rows)

    def active(self, p, i):
        return jnp.logical_and(i >= self.starts[p], i < self.starts[p] + self.tiles[p])

    def row_tile(self, p, i):
        return jnp.clip(i - self.starts[p], 0, self.tiles[p] - 1)

    def col_tile(self, p, i, j, nj):
        return jnp.where(i < self.starts[p], 0, jnp.where(i >= self.starts[p] + self.tiles[p], nj - 1, j))

    def select(self, i, refs):
        x = refs[-1][...]
        for p in range(len(refs) - 2, -1, -1):
            x = jnp.where(i < self.starts[p + 1], refs[p][...], x)
        return x


def _rmsnorm_body(*refs, parts):
    x_refs, (g_ref, o_ref) = refs[:-2], refs[-2:]
    x = parts.select(pl.program_id(0), x_refs)
    r = lax.rsqrt(jnp.mean(x * x, axis=-1, keepdims=True) + EPS)
    o_ref[...] = (x * r * g_ref[...]).astype(o_ref.dtype)


def rmsnorm(xs, g):
    d = xs[0].shape[1]
    rows = [x.shape[0] for x in xs]
    m = sum(rows)
    tm = _tile(math.gcd(*rows) if len(rows) > 1 else m, 256, 8)
    parts = _RowParts(rows, tm)
    return pl.pallas_call(
        functools.partial(_rmsnorm_body, parts=parts),
        out_shape=jax.ShapeDtypeStruct((m, d), BF16),
        grid=(m // tm,),
        in_specs=[pl.BlockSpec((tm, d), lambda i, p=p: (parts.row_tile(p, i), 0)) for p in range(len(xs))]
        + [pl.BlockSpec((1, d), lambda i: (0, 0))],
        out_specs=pl.BlockSpec((tm, d), lambda i: (i, 0)),
        compiler_params=_params("parallel"),
        name="rmsnorm",
    )(*xs, g.reshape(1, d))


def _rmsnorm_router_body(x_ref, g_ref, wr_ref, o_ref, gates_ref, *, n_experts):
    x = x_ref[...]
    r = lax.rsqrt(jnp.mean(x * x, axis=-1, keepdims=True) + EPS)
    h = x * r * g_ref[...]
    o_ref[...] = h.astype(o_ref.dtype)
    logits = jnp.dot(h, wr_ref[...], preferred_element_type=F32, precision=lax.Precision.HIGHEST)
    lane = lax.broadcasted_iota(jnp.int32, logits.shape, 1).astype(F32)
    neg = jnp.float32(-jnp.inf)
    lg = jnp.where(lane < n_experts, logits, neg)
    m1 = jnp.max(lg, axis=-1, keepdims=True)
    i1 = jnp.min(jnp.where(lg == m1, lane, float(LANES)), axis=-1, keepdims=True)
    lg2 = jnp.where(lane == i1, neg, lg)
    m2 = jnp.max(lg2, axis=-1, keepdims=True)
    i2 = jnp.min(jnp.where(lg2 == m2, lane, float(LANES)), axis=-1, keepdims=True)
    e2 = jnp.exp(m2 - m1)
    den = 1.0 + e2
    gates_ref[...] = jnp.where(lane == i1, 1.0 / den, 0.0) + jnp.where(lane == i2, e2 / den, 0.0)


def rmsnorm_router(x, g, w_router):
    m, d = x.shape
    e = w_router.shape[1]
    tm = _tile(m, 256, 8)
    wr = jnp.zeros((d, LANES), F32).at[:, :e].set(w_router)
    return pl.pallas_call(
        functools.partial(_rmsnorm_router_body, n_experts=e),
        out_shape=(jax.ShapeDtypeStruct((m, d), BF16), jax.ShapeDtypeStruct((m, LANES), F32)),
        grid=(m // tm,),
        in_specs=[pl.BlockSpec((tm, d), lambda i: (i, 0)),
                  pl.BlockSpec((1, d), lambda i: (0, 0)),
                  pl.BlockSpec((d, LANES), lambda i: (0, 0))],
        out_specs=(pl.BlockSpec((tm, d), lambda i: (i, 0)),
                   pl.BlockSpec((tm, LANES), lambda i: (i, 0))),
        compiler_params=_params("parallel"),
        name="rmsnorm_router",
    )(x, g.reshape(1, d), wr)


def _matmul_steps(nk, dots, acc_refs, finish):
    if nk == 1:
        finish(dots)
        return
    k = pl.program_id(2)

    @pl.when(k == 0)
    def _():
        for acc in acc_refs:
            acc[...] = jnp.zeros(acc.shape, F32)

    for acc, d in zip(acc_refs, dots):
        acc[...] += d

    @pl.when(k == nk - 1)
    def _():
        finish([acc[...] for acc in acc_refs])


def _mm_tiles(m, n, kd, tm, tn, tk=4096):
    tk = _tile(kd, tk, V7X_MXU_DIM)
    return _tile(m, tm, SUBLANES), _tile(n, tn, LANES), tk, kd // tk


def _acc_scratch(nk, count, tm, tn):
    return [pltpu.VMEM((tm, tn), F32)] * count if nk > 1 else []


def _head_rmsnorm(x, gain):
    return x * lax.rsqrt(jnp.mean(x * x, axis=-1, keepdims=True) + EPS) * gain


def _pipelined(n, produce, consume):
    item = produce(0)
    for c in range(n):
        ahead = produce(c + 1) if c + 1 < n else None
        consume(c, item)
        item = ahead


def _split_norm_refs(rest, norm_dim):
    return (rest[0], rest[1:]) if norm_dim else (None, rest)


def _mm_qkv_body(x_ref, w_ref, gq_ref, gk_ref, *rest, nk, n_q_blocks, n_qk_blocks, q_scale, norm_dim):
    ssq_ref, (o_ref, *acc) = _split_norm_refs(rest, norm_dim)
    j = pl.program_id(1)

    def finish(s):
        is_q = j < n_q_blocks
        is_v = j >= n_qk_blocks
        gain = jnp.where(is_q, gq_ref[...], gk_ref[...])
        post = jnp.where(is_q, jnp.float32(q_scale), jnp.float32(1.0))
        full = s[0] * _row_rms_factor(ssq_ref, norm_dim) if norm_dim else s[0]
        for h in range(o_ref.shape[1] // LANES):
            sl = slice(h * LANES, (h + 1) * LANES)
            x = full[:, sl]
            o_ref[:, sl] = jnp.where(is_v, x, _head_rmsnorm(x, gain) * post).astype(o_ref.dtype)

    _matmul_steps(nk, [jnp.dot(x_ref[...], w_ref[...], preferred_element_type=F32)], acc, finish)


def _mm_qkv_rope_body(x_ref, w_ref, gq_ref, gk_ref, cos_ref, sin_ref, perm_ref, *rest,
                      n_q_blocks, n_qk_blocks, q_scale, norm_dim):
    ssq_ref, (o_ref,) = _split_norm_refs(rest, norm_dim)
    j = pl.program_id(1)
    is_q = j < n_q_blocks
    is_v = j >= n_qk_blocks
    gain = jnp.where(is_q, gq_ref[...], gk_ref[...])
    post = jnp.where(is_q, jnp.float32(q_scale), jnp.float32(1.0))
    rms = _row_rms_factor(ssq_ref, norm_dim) if norm_dim else None
    cos = jnp.concatenate([cos_ref[...]] * 2, axis=1)
    sin = jnp.concatenate([sin_ref[...]] * 2, axis=1)
    x = x_ref[...]
    cw = 2 * LANES
    cols = lambda c: slice(c * cw, (c + 1) * cw)

    def project(c):
        return jnp.dot(x, w_ref[:, cols(c)], preferred_element_type=F32)

    def finish(c, acc):
        if norm_dim:
            acc = acc * rms
        y = jnp.concatenate([_head_rmsnorm(acc[:, :LANES], gain), _head_rmsnorm(acc[:, LANES:], gain)], axis=1)
        rot = jnp.dot(y.astype(perm_ref.dtype), perm_ref[...], preferred_element_type=F32)
        out = (y * cos + rot * sin) * post
        o_ref[:, cols(c)] = jnp.where(is_v, acc, out).astype(o_ref.dtype)

    _pipelined(o_ref.shape[1] // cw, project, finish)


def matmul_qkv(x, w, n_q_cols, n_k_cols, g_q, g_k, q_scale, rope_tables=None, ssq=None, *, tm=1024, tn=1024):
    m, kd = x.shape
    n = w.shape[1]
    hd = g_q.shape[0]
    assert hd == LANES
    n_v_cols = n - n_q_cols - n_k_cols
    tn = _tile(math.gcd(n_q_cols, n_k_cols, n_v_cols), tn, LANES)
    tm, tn, tk, nk = _mm_tiles(m, n, kd, tm, tn)
    in_specs = [pl.BlockSpec((tm, tk), lambda i, j, k: (i, k)),
                pl.BlockSpec((tk, tn), lambda i, j, k: (k, j)),
                pl.BlockSpec((1, hd), lambda i, j, k: (0, 0)),
                pl.BlockSpec((1, hd), lambda i, j, k: (0, 0))]
    args = [x, w, g_q.reshape(1, hd), g_k.reshape(1, hd)]
    blocks = dict(n_q_blocks=n_q_cols // tn, n_qk_blocks=(n_q_cols + n_k_cols) // tn, q_scale=q_scale,
                  norm_dim=kd if ssq is not None else 0)
    if rope_tables is None:
        body = functools.partial(_mm_qkv_body, nk=nk, **blocks)
    else:
        assert nk == 1 and tn % (2 * hd) == 0
        cos, sin, perm = rope_tables
        in_specs += [pl.BlockSpec((tm, hd), lambda i, j, k: (i, 0))] * 2
        in_specs.append(pl.BlockSpec((2 * hd, 2 * hd), lambda i, j, k: (0, 0)))
        args += [cos, sin, perm]
        body = functools.partial(_mm_qkv_rope_body, **blocks)
    if ssq is not None:
        in_specs.append(_ssq_spec(ssq, tm))
        args.append(ssq)
    return pl.pallas_call(
        body,
        out_shape=jax.ShapeDtypeStruct((m, n), BF16),
        grid=(m // tm, n // tn, nk),
        in_specs=in_specs,
        out_specs=pl.BlockSpec((tm, tn), lambda i, j, k: (i, j)),
        scratch_shapes=_acc_scratch(nk, 1, tm, tn),
        compiler_params=_params("parallel", "parallel", "arbitrary"),
        name="matmul_qkv",
    )(*args)


def _row_rms_factor(ssq_ref, d):
    return lax.rsqrt(ssq_ref[...][:, :1] * (1.0 / d) + EPS)


def _ssq_spec(ssq, tm):
    return pl.BlockSpec((tm, ssq.shape[1]), lambda i, j, k: (i, 0))


def _mm_resid_body(*refs, nk, has_bias, r_parts, o_parts, norm_stats):
    n_r, n_o = len(r_parts.tiles), len(o_parts.tiles)
    x_ref, w_ref = refs[:2]
    r_refs = refs[2:2 + n_r]
    b_ref = refs[2 + n_r] if has_bias else None
    o_refs = refs[2 + n_r + has_bias:2 + n_r + has_bias + n_o]
    rest = refs[2 + n_r + has_bias + n_o:]
    (xb_ref, ssq_ref), acc = (rest[:2], rest[2:]) if norm_stats else ((None, None), rest)
    i = pl.program_id(0)

    if norm_stats:
        @pl.when(pl.program_id(1) == 0)
        def _():
            ssq_ref[...] = jnp.zeros(ssq_ref.shape, F32)

    def finish(s):
        out = s[0] + b_ref[...] if has_bias else s[0]
        out = r_parts.select(i, r_refs) + out
        if norm_stats:
            xb_ref[...] = out.astype(xb_ref.dtype)
            ssq_ref[...] += jnp.broadcast_to(jnp.sum(out * out, axis=-1, keepdims=True), ssq_ref.shape)
        if n_o == 1:
            o_refs[0][...] = out
        else:
            for p, o_ref in enumerate(o_refs):
                @pl.when(o_parts.active(p, i))
                def _(o_ref=o_ref):
                    o_ref[...] = out

    _matmul_steps(nk, [jnp.dot(x_ref[...], w_ref[...], preferred_element_type=F32)], acc, finish)


def matmul_residual(x, w, resid, bias=None, out_rows=None, norm_stats=False, *, tm=1024, tn=512):
    m, kd = x.shape
    n = w.shape[1]
    if kd > 4096:
        tm = tm // 2
    r_rows = [r.shape[0] for r in resid]
    o_rows = list(out_rows) if out_rows else [m]
    tm = min(tm, math.gcd(*r_rows, *o_rows))
    tm, tn, tk, nk = _mm_tiles(m, n, kd, tm, tn, tk=8192)
    nj = n // tn
    r_parts, o_parts = _RowParts(r_rows, tm), _RowParts(o_rows, tm)

    def part_spec(parts, p):
        return pl.BlockSpec((tm, tn), lambda i, j, k: (parts.row_tile(p, i), parts.col_tile(p, i, j, nj)))

    in_specs = [pl.BlockSpec((tm, tk), lambda i, j, k: (i, k)),
                pl.BlockSpec((tk, tn), lambda i, j, k: (k, j))]
    in_specs += [part_spec(r_parts, p) for p in range(len(resid))]
    args = [x, w, *resid]
    if bias is not None:
        in_specs.append(pl.BlockSpec((1, tn), lambda i, j, k: (0, j)))
        args.append(bias.reshape(1, n))
    out_shape = [jax.ShapeDtypeStruct((r, n), F32) for r in o_rows]
    out_specs = [part_spec(o_parts, p) for p in range(len(o_rows))]
    if norm_stats:
        assert out_rows is None and nk == 1
        out_shape += [jax.ShapeDtypeStruct((m, n), BF16), jax.ShapeDtypeStruct((m, LANES), F32)]
        out_specs += [pl.BlockSpec((tm, tn), lambda i, j, k: (i, j)),
                      pl.BlockSpec((tm, LANES), lambda i, j, k: (i, 0))]
    out = pl.pallas_call(
        functools.partial(_mm_resid_body, nk=nk, has_bias=bias is not None, r_parts=r_parts, o_parts=o_parts,
                          norm_stats=norm_stats),
        out_shape=out_shape,
        grid=(m // tm, nj, nk),
        in_specs=in_specs,
        out_specs=out_specs,
        scratch_shapes=_acc_scratch(nk, 1, tm, tn),
        compiler_params=(_params("arbitrary", "arbitrary", "arbitrary") if out_rows
                         else _params("parallel", "arbitrary" if norm_stats else "parallel", "arbitrary")),
        name="matmul_residual",
    )(*args)
    return out if (out_rows or norm_stats) else out[0]


def _mm_glu_body(x_ref, wa_ref, wg_ref, ba_ref, bg_ref, *rest, nk, norm_dim):
    ssq_ref, (o_ref, *acc) = _split_norm_refs(rest, norm_dim)

    def finish(s):
        if norm_dim:
            rms = _row_rms_factor(ssq_ref, norm_dim)
            s = [part * rms for part in s]
        a = s[0] + ba_ref[...]
        g = s[1] + bg_ref[...]
        o_ref[...] = (a * jax.nn.sigmoid(g)).astype(o_ref.dtype)

    x = x_ref[...]
    _matmul_steps(nk, [jnp.dot(x, wa_ref[...], preferred_element_type=F32),
                       jnp.dot(x, wg_ref[...], preferred_element_type=F32)], acc, finish)


def matmul_glu(x, w, b, ssq=None, *, tm=1024, tn=512):
    m, kd = x.shape
    n = w.shape[1] // 2
    tm, tn, tk, nk = _mm_tiles(m, n, kd, tm, tn)
    nj = n // tn
    b2 = b.reshape(1, 2 * n)
    return pl.pallas_call(
        functools.partial(_mm_glu_body, nk=nk, norm_dim=kd if ssq is not None else 0),
        out_shape=jax.ShapeDtypeStruct((m, n), BF16),
        grid=(m // tm, nj, nk),
        in_specs=[pl.BlockSpec((tm, tk), lambda i, j, k: (i, k)),
                  pl.BlockSpec((tk, tn), lambda i, j, k: (k, j)),
                  pl.BlockSpec((tk, tn), lambda i, j, k: (k, j + nj)),
                  pl.BlockSpec((1, tn), lambda i, j, k: (0, j)),
                  pl.BlockSpec((1, tn), lambda i, j, k: (0, j + nj))]
        + ([_ssq_spec(ssq, tm)] if ssq is not None else []),
        out_specs=pl.BlockSpec((tm, tn), lambda i, j, k: (i, j)),
        scratch_shapes=_acc_scratch(nk, 2, tm, tn),
        compiler_params=_params("parallel", "parallel", "arbitrary"),
        name="matmul_glu",
    )(x, w, w, b2, b2, *([ssq] if ssq is not None else []))


def _mm_swiglu_body(*refs, nk, gated, norm_dim=0):
    x_ref, w1_ref, w3_ref = refs[:3]
    g_ref = refs[3] if gated else None
    ssq_ref, (o_ref, *acc) = _split_norm_refs(refs[3 + gated:], norm_dim)

    def finish(s):
        if norm_dim:
            rms = _row_rms_factor(ssq_ref, norm_dim)
            s = [part * rms for part in s]
        u = jax.nn.silu(s[0]) * s[1]
        if gated:
            gates = g_ref[...]
            lane = lax.broadcasted_iota(jnp.int32, gates.shape, 1)
            gate = jnp.sum(jnp.where(lane == pl.program_id(1), gates, 0.0), axis=-1, keepdims=True)
            u = u * gate
        o_ref[...] = u.astype(o_ref.dtype)

    x = x_ref[...]
    _matmul_steps(nk, [jnp.dot(x, w1_ref[...], preferred_element_type=F32),
                       jnp.dot(x, w3_ref[...], preferred_element_type=F32)], acc, finish)


def matmul_swiglu(x, w1, w3, ssq=None, *, tm=1024, tn=512):
    m, kd = x.shape
    n = w1.shape[1]
    tm, tn, tk, nk = _mm_tiles(m, n, kd, tm, tn)
    return pl.pallas_call(
        functools.partial(_mm_swiglu_body, nk=nk, gated=False, norm_dim=kd if ssq is not None else 0),
        out_shape=jax.ShapeDtypeStruct((m, n), BF16),
        grid=(m // tm, n // tn, nk),
        in_specs=[pl.BlockSpec((tm, tk), lambda i, j, k: (i, k)),
                  pl.BlockSpec((tk, tn), lambda i, j, k: (k, j)),
                  pl.BlockSpec((tk, tn), lambda i, j, k: (k, j))]
        + ([_ssq_spec(ssq, tm)] if ssq is not None else []),
        out_specs=pl.BlockSpec((tm, tn), lambda i, j, k: (i, j)),
        scratch_shapes=_acc_scratch(nk, 2, tm, tn),
        compiler_params=_params("parallel", "parallel", "arbitrary"),
        name="matmul_swiglu",
    )(x, w1, w3, *([ssq] if ssq is not None else []))


def matmul_swiglu_experts(x, w1, w3, gates, *, tm=1024):
    m, kd = x.shape
    e, _, f = w1.shape
    tm, _, tk, nk = _mm_tiles(m, f, kd, tm, f)
    return pl.pallas_call(
        functools.partial(_mm_swiglu_body, nk=nk, gated=True),
        out_shape=jax.ShapeDtypeStruct((m, e * f), BF16),
        grid=(m // tm, e, nk),
        in_specs=[pl.BlockSpec((tm, tk), lambda i, j, k: (i, k)),
                  pl.BlockSpec((None, tk, f), lambda i, j, k: (j, k, 0)),
                  pl.BlockSpec((None, tk, f), lambda i, j, k: (j, k, 0)),
                  pl.BlockSpec((tm, LANES), lambda i, j, k: (i, 0))],
        out_specs=pl.BlockSpec((tm, f), lambda i, j, k: (i, j)),
        scratch_shapes=_acc_scratch(nk, 2, tm, f),
        compiler_params=_params("parallel", "parallel", "arbitrary"),
        name="matmul_swiglu_experts",
    )(x, w1, w3, gates)


def _rope_tables(seq_lens):
    half = LANES // 2
    inv_freq = ROPE_THETA ** (-jnp.arange(0, half, 2, dtype=F32) / half)
    cos_l, sin_l = [], []
    for s in seq_lens:
        t = jnp.arange(s, dtype=jnp.int32)
        row = (t // GRID_W).astype(F32)
        col = (t % GRID_W).astype(F32)
        ang_r = row[:, None] * inv_freq[None, :]
        ang_c = col[:, None] * inv_freq[None, :]
        emb = jnp.concatenate([ang_r, ang_r, ang_c, ang_c], axis=-1)
        cos_l.append(jnp.cos(emb))
        sin_l.append(jnp.sin(emb))
    cos = jnp.concatenate(cos_l, axis=0)
    sin = jnp.concatenate(sin_l, axis=0)
    quarter = half // 2
    perm = np.zeros((2 * LANES, 2 * LANES), np.float32)
    for i in range(2 * LANES):
        if i % half < quarter:
            perm[i + quarter, i] = -1.0
        else:
            perm[i - quarter, i] = 1.0
    return cos, sin, jnp.asarray(perm, BF16)


def _transpose_body(x_ref, o_ref):
    o_ref[...] = x_ref[...].astype(F32).T.astype(o_ref.dtype)


def transpose_cols(x, col0, ncols, *, tm=512, bw=512):
    m = x.shape[0]
    tm = _tile(m, tm, LANES)
    bw = _tile(math.gcd(ncols, col0) if col0 else ncols, bw, LANES)
    c0 = col0 // bw
    return pl.pallas_call(
        _transpose_body,
        out_shape=jax.ShapeDtypeStruct((ncols, m), x.dtype),
        grid=(m // tm, ncols // bw),
        in_specs=[pl.BlockSpec((tm, bw), lambda i, j: (i, c0 + j))],
        out_specs=pl.BlockSpec((bw, tm), lambda i, j: (j, i)),
        compiler_params=_params("parallel", "parallel"),
        name="transpose_cols",
    )(x)


def _step_tables(seq_lens, tq, tk):
    qt, kt, first, last, qrel, krel, side = [], [], [], [], [], [], []
    start = 0
    for s in seq_lens:
        for qi in range(s // tq):
            nkv = s // tk
            for ki in range(nkv):
                qt.append(start // tq + qi)
                kt.append(start // tk + ki)
                first.append(int(ki == 0))
                last.append(int(ki == nkv - 1))
                qrel.append(qi * tq)
                krel.append(ki * tk)
                side.append(1 if (ki + 1) * tk <= qi * tq else (-1 if ki * tk >= (qi + 1) * tq else 0))
        start += s
    return [jnp.asarray(np.asarray(a, np.int32)) for a in (qt, kt, first, last, qrel, krel, side)]


ONES_ROWS = 16


def _with_ones_rows(vt):
    return jnp.concatenate([vt, jnp.ones((ONES_ROWS, vt.shape[1]), vt.dtype)], axis=0)


def _online_softmax_step(st, vt1, m_ref, acc_ref, idx, shift=None):
    m_prev = m_ref[idx]
    tile_max = jnp.max(st, axis=0, keepdims=True)
    if shift is not None:
        tile_max = tile_max + shift
    m_new = jnp.maximum(m_prev, tile_max)
    alpha = jnp.exp2(m_prev - m_new)
    p = jnp.exp2(st - (m_new if shift is None else m_new - shift))
    acc_ref[idx] = alpha * acc_ref[idx] + jnp.dot(vt1, p.astype(vt1.dtype), preferred_element_type=F32)
    m_ref[idx] = m_new


SAFE_LOG2 = 60.0


def _score_bound(g_q, g_k, q_scale):
    bound = q_scale * LANES * jnp.max(jnp.abs(g_q)) * jnp.max(jnp.abs(g_k))
    return (bound <= SAFE_LOG2).astype(jnp.int32).reshape(1)


def _pv_fixed_reference(st, vt1):
    return jnp.dot(vt1, jnp.exp2(st).astype(vt1.dtype), preferred_element_type=F32)


def _scores_t(k, q):
    return lax.dot_general(k, q, (((1,), (1,)), ((), ())), preferred_element_type=F32)


def _init_softmax_state(m_ref, acc_ref):
    m_ref[...] = jnp.full(m_ref.shape, -jnp.inf, F32)
    acc_ref[...] = jnp.zeros(acc_ref.shape, F32)


def _gqa_body(qt_ref, kt_ref, first_ref, last_ref, qrel_ref, krel_ref, side_ref, bounded_ref,
              q_ref, k_ref, vt_ref, o_ref, m_ref, acc_ref, *, group, kc):
    s = pl.program_id(1)

    @pl.when(first_ref[s] == 1)
    def _():
        _init_softmax_state(m_ref, acc_ref)

    def scores(c):
        k = k_ref[c * kc:(c + 1) * kc, :]
        return [_scores_t(k, q_ref[:, g * LANES:(g + 1) * LANES]) for g in range(group)]

    @pl.when(bounded_ref[0] == 1)
    def _():
        pv = [[] for _ in range(group)]

        def softmax(c, sts):
            vt1 = _with_ones_rows(vt_ref[:, c * kc:(c + 1) * kc])
            for g in range(group):
                pv[g].append(_pv_fixed_reference(sts[g], vt1))

        _pipelined(k_ref.shape[0] // kc, scores, softmax)
        for g in range(group):
            acc_ref[g] += functools.reduce(lambda a, b: a + b, pv[g])

    @pl.when(bounded_ref[0] == 0)
    def _():
        def softmax(c, sts):
            vt1 = _with_ones_rows(vt_ref[:, c * kc:(c + 1) * kc])
            for g in range(group):
                _online_softmax_step(sts[g], vt1, m_ref, acc_ref, g)

        _pipelined(k_ref.shape[0] // kc, scores, softmax)

    @pl.when(last_ref[s] == 1)
    def _():
        for g in range(group):
            acc = acc_ref[g]
            o = acc[:LANES] / acc[LANES:LANES + 1]
            o_ref[:, g * LANES:(g + 1) * LANES] = o.T.astype(o_ref.dtype)


def gqa_attention(qk, vt, bounded, seq_lens, n_heads, n_kv, *, tq=1024, tk=2048, kc=256):
    m = qk.shape[0]
    group = n_heads // n_kv
    g_all = math.gcd(*seq_lens)
    tq, tk = _tile(g_all, tq, LANES), _tile(g_all, tk, LANES)
    tabs = _step_tables(seq_lens, tq, tk)
    n_steps = tabs[0].shape[0]
    return pl.pallas_call(
        functools.partial(_gqa_body, group=group, kc=_tile(tk, kc, LANES)),
        out_shape=jax.ShapeDtypeStruct((m, n_heads * LANES), BF16),
        grid_spec=pltpu.PrefetchScalarGridSpec(
            num_scalar_prefetch=8,
            grid=(n_kv, n_steps),
            in_specs=[pl.BlockSpec((tq, group * LANES), lambda h, s, qt, kt, *_: (qt[s], h)),
                      pl.BlockSpec((tk, LANES), lambda h, s, qt, kt, *_: (kt[s], n_heads + h)),
                      pl.BlockSpec((LANES, tk), lambda h, s, qt, kt, *_: (h, kt[s]))],
            out_specs=pl.BlockSpec((tq, group * LANES), lambda h, s, qt, kt, *_: (qt[s], h)),
            scratch_shapes=[pltpu.VMEM((group, 1, tq), F32),
                            pltpu.VMEM((group, LANES + ONES_ROWS, tq), F32)]),
        compiler_params=_params("parallel", "arbitrary"),
        name="gqa_attention",
    )(*tabs, bounded, qk, qk, vt)


ALIBI_SPLIT = 3


def _diff_body(qt_ref, kt_ref, first_ref, last_ref, qrel_ref, krel_ref, side_ref, bounded_ref, sgn_ref, slope_ref,
               q_ref, k_ref, vt_ref, kpos_ref, ext_ref, lq1_ref, lk1_ref, lq2_ref, lk2_ref, gsub_ref,
               o_ref, m_ref, acc_ref, *, lambda_init, kc):
    h = pl.program_id(0)
    s = pl.program_id(1)

    @pl.when(first_ref[s] == 1)
    def _():
        _init_softmax_state(m_ref, acc_ref)

    tq, tk = q_ref.shape[0], k_ref.shape[0]
    dv = vt_ref.shape[0]
    slope2 = slope_ref[h]
    subs = [slice(c * LANES, (c + 1) * LANES) for c in range(2)]
    rows = lambda c: slice(c * kc, (c + 1) * kc)

    def online(scores, shift=None):
        def softmax(c, sts):
            vt1 = _with_ones_rows(vt_ref[:, rows(c)])
            for sub in range(2):
                _online_softmax_step(sts[sub], vt1, m_ref, acc_ref, sub, shift)

        _pipelined(tk // kc, scores, softmax)

    def fixed_reference(scores):
        pv = [[], []]

        def softmax(c, sts):
            vt1 = _with_ones_rows(vt_ref[:, rows(c)])
            for sub in range(2):
                pv[sub].append(_pv_fixed_reference(sts[sub], vt1))

        _pipelined(tk // kc, scores, softmax)
        for sub in range(2):
            acc_ref[sub] += functools.reduce(lambda a, b: a + b, pv[sub])

    bounded = bounded_ref[0] == 1

    @pl.when(side_ref[s] == 0)
    def _():
        keys = lax.broadcasted_iota(jnp.int32, (kc, tq), 0)
        queries = lax.broadcasted_iota(jnp.int32, (kc, tq), 1)
        delta = keys - queries + (krel_ref[s] - qrel_ref[s])

        def scores(c):
            bias = jnp.abs(delta + c * kc).astype(F32) * (-slope2)
            return [_scores_t(k_ref[rows(c), sl], q_ref[:, sl]) + bias for sl in subs]

        pl.when(bounded)(lambda: fixed_reference(scores))
        pl.when(jnp.logical_not(bounded))(lambda: online(scores))

    @pl.when(side_ref[s] != 0)
    def _():
        sgn = sgn_ref[s]
        ext_row = ext_ref[...] * sgn

        def augmented(q_ext):
            q_aug = [jnp.concatenate([q_ref[:, sl], q_ext.astype(q_ref.dtype)], axis=1) for sl in subs]

            def scores(c):
                k_ext = kpos_ref[rows(c), :]
                return [_scores_t(jnp.concatenate([k_ref[rows(c), sl], k_ext], axis=1), q_aug[sub])
                        for sub, sl in enumerate(subs)]

            return scores

        @pl.when(bounded)
        def _():
            qpos = qrel_ref[s] + lax.broadcasted_iota(jnp.int32, (tq, LANES), 0)
            rest = (sgn * slope2) * (krel_ref[s] - qpos).astype(F32)
            lane = lax.broadcasted_iota(jnp.int32, (tq, LANES), 1)
            q_ext = jnp.broadcast_to(ext_row, (tq, LANES))
            for i in range(ALIBI_SPLIT):
                piece = rest.astype(BF16).astype(F32)
                q_ext = jnp.where(lane == 2 * ALIBI_SPLIT + i, piece, q_ext)
                rest = rest - piece
            fixed_reference(augmented(q_ext))

        @pl.when(jnp.logical_not(bounded))
        def _():
            qpos = qrel_ref[s] + lax.broadcasted_iota(jnp.int32, (1, tq), 1)
            shift = (sgn * slope2) * (krel_ref[s] - qpos).astype(F32)
            online(augmented(jnp.broadcast_to(ext_row, (tq, LANES))), shift)

    @pl.when(last_ref[s] == 1)
    def _():
        lam = (jnp.exp(jnp.sum(lq1_ref[...] * lk1_ref[...], axis=-1, keepdims=True))
               - jnp.exp(jnp.sum(lq2_ref[...] * lk2_ref[...], axis=-1, keepdims=True)) + lambda_init)
        acc0, acc1 = acc_ref[0], acc_ref[1]
        o = (acc0[:dv] * (1.0 / acc0[dv:dv + 1])
             - acc1[:dv] * (lam * (1.0 / acc1[dv:dv + 1])))
        r = lax.rsqrt(jnp.mean(o * o, axis=0, keepdims=True) + EPS)
        o_ref[...] = (((o * r).T * gsub_ref[...]) * (1.0 - lambda_init)).astype(o_ref.dtype)


def _alibi_tables(n_heads, tk):
    slope2 = (2.0 ** (-8.0 * np.arange(1, n_heads + 1, dtype=np.float64) / n_heads) * LOG2E).astype(np.float32)
    to_bf16 = lambda a: a.astype(BF16).astype(np.float32)
    pieces, rest = [], slope2.copy()
    for _ in range(ALIBI_SPLIT):
        c = to_bf16(rest)
        pieces.append(c)
        rest = (rest - c).astype(np.float32)
    ext = np.zeros((n_heads, 1, LANES), np.float32)
    kpos = np.zeros((tk, LANES), np.float32)
    j = np.arange(tk)
    for i, c in enumerate(pieces):
        ext[:, 0, i] = float(LANES) * c
        ext[:, 0, ALIBI_SPLIT + i] = c
        kpos[:, i] = j // LANES
        kpos[:, ALIBI_SPLIT + i] = j % LANES
        kpos[:, 2 * ALIBI_SPLIT + i] = 1.0
    assert tk // LANES <= 256, "key offsets must stay exact in bf16"
    return jnp.asarray(slope2), jnp.asarray(ext), jnp.asarray(kpos, BF16)


def diff_attention(qk, vt, bounded, seq_lens, n_heads, lq1, lk1, lq2, lk2, g_sub, lambda_init,
                   *, tq=1024, tk=2048, kc=256):
    m = qk.shape[0]
    dv = 2 * LANES
    g_all = math.gcd(*seq_lens)
    tq, tk = _tile(g_all, tq, LANES), _tile(g_all, tk, LANES)
    tabs = _step_tables(seq_lens, tq, tk)
    n_steps = tabs[0].shape[0]
    sgn = tabs[-1].astype(F32)
    slope2, ext, kpos = _alibi_tables(n_heads, tk)
    vec = lambda a: a.reshape(1, -1)
    small = pl.BlockSpec((1, LANES), lambda h, s, *_: (0, 0))
    return pl.pallas_call(
        functools.partial(_diff_body, lambda_init=lambda_init, kc=_tile(tk, kc, LANES)),
        out_shape=jax.ShapeDtypeStruct((m, n_heads * dv), BF16),
        grid_spec=pltpu.PrefetchScalarGridSpec(
            num_scalar_prefetch=10,
            grid=(n_heads, n_steps),
            in_specs=[pl.BlockSpec((tq, dv), lambda h, s, qt, kt, *_: (qt[s], h)),
                      pl.BlockSpec((tk, dv), lambda h, s, qt, kt, *_: (kt[s], n_heads + h)),
                      pl.BlockSpec((dv, tk), lambda h, s, qt, kt, *_: (h, kt[s])),
                      pl.BlockSpec((tk, LANES), lambda h, s, *_: (0, 0)),
                      pl.BlockSpec((None, 1, LANES), lambda h, s, *_: (h, 0, 0)),
                      small, small, small, small,
                      pl.BlockSpec((1, dv), lambda h, s, *_: (0, 0))],
            out_specs=pl.BlockSpec((tq, dv), lambda h, s, qt, kt, *_: (qt[s], h)),
            scratch_shapes=[pltpu.VMEM((2, 1, tq), F32),
                            pltpu.VMEM((2, dv + ONES_ROWS, tq), F32)]),
        compiler_params=_params("parallel", "arbitrary"),
        name="diff_attention",
    )(*tabs, bounded, sgn, slope2, qk, qk, vt, kpos, ext, vec(lq1), vec(lk1), vec(lq2), vec(lk2), vec(g_sub))


HALO = 16


def _dwconv_ln_silu_body(hp_ref, hn_ref, prev_ref, cur_ref, next_ref, w_ref, b_ref, g_ref, beta_ref,
                         o_ref, win_ref, conv_ref, *, width, lc):
    i = pl.program_id(0)
    tt, d = cur_ref.shape
    pad = width // 2
    zeros = jnp.zeros(prev_ref.shape, F32)
    win_ref[0:HALO, :] = jnp.where(hp_ref[i] == 1, prev_ref[...].astype(F32), zeros)
    win_ref[HALO:HALO + tt, :] = cur_ref[...].astype(F32)
    win_ref[HALO + tt:, :] = jnp.where(hn_ref[i] == 1, next_ref[...].astype(F32), zeros)
    rc = 64
    first = HALO - pad
    span = rc + SUBLANES * ((width - 1) // SUBLANES) + SUBLANES

    def lane_chunk(c, carry):
        lanes = pl.ds(pl.multiple_of(c * lc, lc), lc)
        for r0 in range(0, tt, rc):
            acc = jnp.zeros((rc, lc), F32) + b_ref[:, lanes]
            block = win_ref[r0:r0 + span, lanes]
            for phase in range(SUBLANES):
                shift = (first + phase) % SUBLANES
                rolled = block if shift == 0 else pltpu.roll(block, span - shift, 0)
                for t in range(phase, width, SUBLANES):
                    a = first + t - shift
                    acc = acc + rolled[a:a + rc, :] * w_ref[t:t + 1, lanes]
            conv_ref[r0:r0 + rc, lanes] = acc
        return carry

    lax.fori_loop(0, d // lc, lane_chunk, 0)

    x = conv_ref[...]
    mu = jnp.mean(x, axis=-1, keepdims=True)
    xc = x - mu
    var = jnp.mean(xc * xc, axis=-1, keepdims=True)
    y = xc * lax.rsqrt(var + EPS) * g_ref[...] + beta_ref[...]
    o_ref[...] = jax.nn.silu(y).astype(o_ref.dtype)


def dwconv_ln_silu(u, w_dw, b_dw, ln_g, ln_b, seq_lens, *, tt=256, lc=512):
    m, d = u.shape
    width = w_dw.shape[0]
    assert width // 2 <= HALO
    tt = _tile(math.gcd(*seq_lens), tt, 64)
    lc = _tile(d, lc, LANES)
    nb = tt // HALO
    starts = np.cumsum([0] + list(seq_lens))
    has_prev = np.ones(m // tt, np.int32)
    has_next = np.ones(m // tt, np.int32)
    for st in starts[:-1]:
        has_prev[st // tt] = 0
    for en in starts[1:]:
        has_next[en // tt - 1] = 0
    last_halo = m // HALO - 1
    row = lambda a: a.reshape(1, d)
    vec_spec = pl.BlockSpec((1, d), lambda i, *_: (0, 0))
    return pl.pallas_call(
        functools.partial(_dwconv_ln_silu_body, width=width, lc=lc),
        out_shape=jax.ShapeDtypeStruct((m, d), BF16),
        grid_spec=pltpu.PrefetchScalarGridSpec(
            num_scalar_prefetch=2,
            grid=(m // tt,),
            in_specs=[pl.BlockSpec((HALO, d), lambda i, *_: (jnp.maximum(i * nb - 1, 0), 0)),
                      pl.BlockSpec((tt, d), lambda i, *_: (i, 0)),
                      pl.BlockSpec((HALO, d), lambda i, *_: (jnp.minimum((i + 1) * nb, last_halo), 0)),
                      pl.BlockSpec((width, d), lambda i, *_: (0, 0)),
                      vec_spec, vec_spec, vec_spec],
            out_specs=pl.BlockSpec((tt, d), lambda i, *_: (i, 0)),
            scratch_shapes=[pltpu.VMEM((tt + 2 * HALO, d), F32), pltpu.VMEM((tt, d), F32)]),
        compiler_params=_params("parallel"),
        name="dwconv_ln_silu",
    )(jnp.asarray(has_prev), jnp.asarray(has_next), u, u, u, w_dw, row(b_dw), row(ln_g), row(ln_b))


def _cast_body(*refs, scaled):
    if scaled:
        w_ref, g_ref, o_ref = refs
        o_ref[...] = (w_ref[...] * g_ref[...]).astype(o_ref.dtype)
    else:
        w_ref, o_ref = refs
        o_ref[...] = w_ref[...].astype(o_ref.dtype)


def cast_layer(w, idx, row_scale=None, *, block_bytes=4 * 1024 * 1024):
    lead, tail = w.shape[0], w.shape[1:]
    cols = tail[-1]
    rows = math.prod(tail[:-1])
    tc = _tile(cols, 2048, LANES)
    tr = _tile(rows, max(16, block_bytes // (4 * tc)), 16)
    in_specs = [pl.BlockSpec((None, tr, tc), lambda i, j: (idx, i, j))]
    args = [w.reshape(lead, rows, cols)]
    if row_scale is not None:
        in_specs.append(pl.BlockSpec((tr, 1), lambda i, j: (i, 0)))
        args.append(row_scale.reshape(rows, 1))
    out = pl.pallas_call(
        functools.partial(_cast_body, scaled=row_scale is not None),
        out_shape=jax.ShapeDtypeStruct((rows, cols), BF16),
        grid=(rows // tr, cols // tc),
        in_specs=in_specs,
        out_specs=pl.BlockSpec((tr, tc), lambda i, j: (i, j)),
        compiler_params=_params("parallel", "parallel"),
        name="cast_layer",
    )(*args)
    return out.reshape(tail)


def kernel(x_prompt, x_sample, norm_mix, norm_ffn, a_w_qkv, a_w_o, a_q_norm, a_k_norm, b_w_pw1, b_b_pw1, b_w_dw, b_b_dw, b_ln_g, b_ln_b, b_w_pw2, b_b_pw2, c_w_qkv, c_w_o, c_q_norm, c_k_norm, c_lambda_q1, c_lambda_k1, c_lambda_q2, c_lambda_k2, c_subln, ffn_w1, ffn_w3, ffn_w2, moe_router, moe_w1, moe_w3, moe_w2):
    depth, d = norm_mix.shape
    bp, sp, _ = x_prompt.shape
    bs, ss, _ = x_sample.shape
    seq_lens = [sp] * bp + [ss] * bs
    mp, ms = bp * sp, bs * ss
    x = [x_prompt.reshape(mp, d), x_sample.reshape(ms, d)]

    hd = a_q_norm.shape[-1]
    a_heads = d // hd
    a_kv = (a_w_qkv.shape[-1] - d) // (2 * hd)
    c_heads = d // (2 * c_q_norm.shape[-1])
    n_exp, _, f_exp = moe_w1.shape[1:]
    rope_tables = _rope_tables(seq_lens)
    bf = cast_layer

    def residual(u, w, x, bias=None, want_stats=False, out_rows=None):
        if want_stats:
            x_new, xb, ssq = matmul_residual(u, w, x, bias, norm_stats=True)
            return [x_new], (xb, ssq)
        out = matmul_residual(u, w, x, bias, out_rows=out_rows)
        return (out if out_rows else [out]), None

    def normed_input(x, stats, gain):
        if stats is None:
            return rmsnorm(x, gain), None, None
        return stats[0], stats[1], gain

    stats = None
    for i in range(depth):
        h, ssq, gain = normed_input(x, stats, norm_mix[i])
        mixer, j = i % N_MIXERS, i // N_MIXERS
        dense_ffn = i % 2 == 0
        if mixer == 0:
            q_scale = hd ** -0.5 * LOG2E
            qkv = matmul_qkv(h, bf(a_w_qkv, j, gain), d, a_kv * hd, a_q_norm[j], a_k_norm[j], q_scale,
                             rope_tables, ssq)
            vt = transpose_cols(qkv, d + a_kv * hd, a_kv * hd)
            bounded = _score_bound(a_q_norm[j], a_k_norm[j], q_scale)
            o = gqa_attention(qkv, vt, bounded, seq_lens, a_heads, a_kv)
            x, stats = residual(o, bf(a_w_o, j), x, want_stats=dense_ffn)
        elif mixer == 1:
            u = matmul_glu(h, bf(b_w_pw1, j, gain), b_b_pw1[j], ssq)
            u = dwconv_ln_silu(u, b_w_dw[j], b_b_dw[j], b_ln_g[j], b_ln_b[j], seq_lens)
            x, stats = residual(u, bf(b_w_pw2, j), x, b_b_pw2[j], want_stats=dense_ffn)
        else:
            lambda_init = 0.8 - 0.6 * math.exp(-0.3 * i)
            q_scale = c_q_norm.shape[-1] ** -0.5 * LOG2E
            qkv = matmul_qkv(h, bf(c_w_qkv, j, gain), d, d, c_q_norm[j], c_k_norm[j], q_scale, None, ssq)
            vt = transpose_cols(qkv, 2 * d, d)
            bounded = _score_bound(c_q_norm[j], c_k_norm[j], q_scale)
            o = diff_attention(qkv, vt, bounded, seq_lens, c_heads, c_lambda_q1[j], c_lambda_k1[j],
                               c_lambda_q2[j], c_lambda_k2[j], c_subln[j], lambda_init)
            x, stats = residual(o, bf(c_w_o, j), x, want_stats=dense_ffn)
        k = i // 2
        last = i == depth - 1
        out_rows = [mp, ms] if last else None
        if dense_ffn:
            h, ssq, gain = normed_input(x, stats, norm_ffn[i])
            u = matmul_swiglu(h, bf(ffn_w1, k, gain), bf(ffn_w3, k, gain), ssq)
            x, stats = residual(u, bf(ffn_w2, k), x, want_stats=not last, out_rows=out_rows)
        else:
            h, gates = rmsnorm_router(x[0], norm_ffn[i], moe_router[k])
            u = matmul_swiglu_experts(h, bf(moe_w1, k), bf(moe_w3, k), gates)
            x, stats = residual(u, bf(moe_w2, k).reshape(n_exp * f_exp, d), x, want_stats=not last,
                                out_rows=out_rows)

    return (x[0].reshape(bp, sp, d), x[1].reshape(bs, ss, d))
```

```python
import functools
import math

import jax
import jax.numpy as jnp
import numpy as np
from jax import lax
from jax.experimental import pallas as pl
from jax.experimental.pallas import tpu as pltpu

EPS = 1e-6
ROPE_THETA = 10000.0
GRID_W = 64
N_MIXERS = 3
LOG2E = 1.4426950408889634

LANES = 128
SUBLANES = 8
V7X_MXU_DIM = 256
V7X_VMEM_LIMIT_BYTES = 56 * 1024 * 1024

F32 = jnp.float32
BF16 = jnp.bfloat16


def _tile(dim, pref, align):
    if dim <= pref:
        return dim
    t = (pref // align) * align
    while t >= align:
        if dim % t == 0:
            return t
        t -= align
    return dim


def _params(*sem):
    return pltpu.CompilerParams(dimension_semantics=sem, vmem_limit_bytes=V7X_VMEM_LIMIT_BYTES)


class _RowParts:
    def __init__(self, rows, tm):
        self.tiles = [r // tm for r in rows]
        self.starts = [sum(self.tiles[:p]) for p in range(len(rows))]
        assert all(r % tm == 0 for r in rows)

    def active(self, p, i):
        return jnp.logical_and(i >= self.starts[p], i < self.starts[p] + self.tiles[p])

    def row_tile(self, p, i):
        return jnp.clip(i - self.starts[p], 0, self.tiles[p] - 1)

    def col_tile(self, p, i, j, nj):
        return jnp.where(i < self.starts[p], 0, jnp.where(i >= self.starts[p] + self.tiles[p], nj - 1, j))

    def select(self, i, refs):
        x = refs[-1][...]
        for p in range(len(refs) - 2, -1, -1):
            x = jnp.where(i < self.starts[p + 1], refs[p][...], x)
        return x


def _rmsnorm_body(*refs, parts):
    x_refs, (g_ref, o_ref) = refs[:-2], refs[-2:]
    x = parts.select(pl.program_id(0), x_refs)
    r = lax.rsqrt(jnp.mean(x * x, axis=-1, keepdims=True) + EPS)
    o_ref[...] = (x * r * g_ref[...]).astype(o_ref.dtype)


def rmsnorm(xs, g):
    d = xs[0].shape[1]
    rows = [x.shape[0] for x in xs]
    m = sum(rows)
    tm = _tile(math.gcd(*rows) if len(rows) > 1 else m, 256, 8)
    parts = _RowParts(rows, tm)
    return pl.pallas_call(
        functools.partial(_rmsnorm_body, parts=parts),
        out_shape=jax.ShapeDtypeStruct((m, d), BF16),
        grid=(m // tm,),
        in_specs=[pl.BlockSpec((tm, d), lambda i, p=p: (parts.row_tile(p, i), 0)) for p in range(len(xs))]
        + [pl.BlockSpec((1, d), lambda i: (0, 0))],
        out_specs=pl.BlockSpec((tm, d), lambda i: (i, 0)),
        compiler_params=_params("parallel"),
        name="rmsnorm",
    )(*xs, g.reshape(1, d))


def _rmsnorm_router_body(x_ref, g_ref, wr_ref, o_ref, gates_ref, *, n_experts):
    x = x_ref[...]
    r = lax.rsqrt(jnp.mean(x * x, axis=-1, keepdims=True) + EPS)
    h = x * r * g_ref[...]
    o_ref[...] = h.astype(o_ref.dtype)
    w = wr_ref[...]
    h_hi, w_hi = h.astype(BF16), w.astype(BF16)
    h_lo, w_lo = (h - h_hi.astype(F32)).astype(BF16), (w - w_hi.astype(F32)).astype(BF16)
    dot = functools.partial(jnp.dot, preferred_element_type=F32)
    logits = dot(h_hi, w_hi) + (dot(h_hi, w_lo) + dot(h_lo, w_hi))
    lane = lax.broadcasted_iota(jnp.int32, logits.shape, 1).astype(F32)
    neg = jnp.float32(-jnp.inf)
    lg = jnp.where(lane < n_experts, logits, neg)
    m1 = jnp.max(lg, axis=-1, keepdims=True)
    i1 = jnp.min(jnp.where(lg == m1, lane, float(LANES)), axis=-1, keepdims=True)
    lg2 = jnp.where(lane == i1, neg, lg)
    m2 = jnp.max(lg2, axis=-1, keepdims=True)
    i2 = jnp.min(jnp.where(lg2 == m2, lane, float(LANES)), axis=-1, keepdims=True)
    e2 = jnp.exp(m2 - m1)
    den = 1.0 + e2
    gates_ref[...] = jnp.where(lane == i1, 1.0 / den, 0.0) + jnp.where(lane == i2, e2 / den, 0.0)


def rmsnorm_router(x, g, w_router):
    m, d = x.shape
    e = w_router.shape[1]
    tm = _tile(m, 256, 8)
    wr = jnp.zeros((d, LANES), F32).at[:, :e].set(w_router)
    return pl.pallas_call(
        functools.partial(_rmsnorm_router_body, n_experts=e),
        out_shape=(jax.ShapeDtypeStruct((m, d), BF16), jax.ShapeDtypeStruct((m, LANES), F32)),
        grid=(m // tm,),
        in_specs=[pl.BlockSpec((tm, d), lambda i: (i, 0)),
                  pl.BlockSpec((1, d), lambda i: (0, 0)),
                  pl.BlockSpec((d, LANES), lambda i: (0, 0))],
        out_specs=(pl.BlockSpec((tm, d), lambda i: (i, 0)),
                   pl.BlockSpec((tm, LANES), lambda i: (i, 0))),
        compiler_params=_params("parallel"),
        name="rmsnorm_router",
    )(x, g.reshape(1, d), wr)


def _matmul_steps(nk, dots, acc_refs, finish):
    if nk == 1:
        finish(dots)
        return
    k = pl.program_id(2)

    @pl.when(k == 0)
    def _():
        for acc in acc_refs:
            acc[...] = jnp.zeros(acc.shape, F32)

    for acc, d in zip(acc_refs, dots):
        acc[...] += d

    @pl.when(k == nk - 1)
    def _():
        finish([acc[...] for acc in acc_refs])


def _mm_tiles(m, n, kd, tm, tn, tk=4096):
    tk = _tile(kd, tk, V7X_MXU_DIM)
    return _tile(m, tm, SUBLANES), _tile(n, tn, LANES), tk, kd // tk


def _acc_scratch(nk, count, tm, tn):
    return [pltpu.VMEM((tm, tn), F32)] * count if nk > 1 else []


def _head_rmsnorm(x, gain):
    return x * lax.rsqrt(jnp.mean(x * x, axis=-1, keepdims=True) + EPS) * gain


def _pipelined(n, produce, consume):
    item = produce(0)
    for c in range(n):
        ahead = produce(c + 1) if c + 1 < n else None
        consume(c, item)
        item = ahead


def _split_norm_refs(rest, norm_dim):
    return (rest[0], rest[1:]) if norm_dim else (None, rest)


def _mm_qkv_body(x_ref, w_ref, gq_ref, gk_ref, *rest, nk, n_q_blocks, n_qk_blocks, q_scale, norm_dim):
    ssq_ref, (o_ref, *acc) = _split_norm_refs(rest, norm_dim)
    j = pl.program_id(1)

    def finish(s):
        is_q = j < n_q_blocks
        is_v = j >= n_qk_blocks
        gain = jnp.where(is_q, gq_ref[...], gk_ref[...])
        post = jnp.where(is_q, jnp.float32(q_scale), jnp.float32(1.0))
        full = s[0] * _row_rms_factor(ssq_ref, norm_dim) if norm_dim else s[0]
        for h in range(o_ref.shape[1] // LANES):
            sl = slice(h * LANES, (h + 1) * LANES)
            x = full[:, sl]
            o_ref[:, sl] = jnp.where(is_v, x, _head_rmsnorm(x, gain) * post).astype(o_ref.dtype)

    _matmul_steps(nk, [jnp.dot(x_ref[...], w_ref[...], preferred_element_type=F32)], acc, finish)


def _mm_qkv_rope_body(x_ref, w_ref, gq_ref, gk_ref, cos_ref, sin_ref, perm_ref, *rest,
                      n_q_blocks, n_qk_blocks, q_scale, norm_dim):
    ssq_ref, (o_ref,) = _split_norm_refs(rest, norm_dim)
    j = pl.program_id(1)
    is_q = j < n_q_blocks
    is_v = j >= n_qk_blocks
    gain = jnp.where(is_q, gq_ref[...], gk_ref[...])
    post = jnp.where(is_q, jnp.float32(q_scale), jnp.float32(1.0))
    rms = _row_rms_factor(ssq_ref, norm_dim) if norm_dim else None
    cos = jnp.concatenate([cos_ref[...]] * 2, axis=1)
    sin = jnp.concatenate([sin_ref[...]] * 2, axis=1)
    x = x_ref[...]
    cw = 2 * LANES
    cols = lambda c: slice(c * cw, (c + 1) * cw)

    def project(c):
        return jnp.dot(x, w_ref[:, cols(c)], preferred_element_type=F32)

    def finish(c, acc):
        if norm_dim:
            acc = acc * rms
        y = jnp.concatenate([_head_rmsnorm(acc[:, :LANES], gain), _head_rmsnorm(acc[:, LANES:], gain)], axis=1)
        rot = jnp.dot(y.astype(perm_ref.dtype), perm_ref[...], preferred_element_type=F32)
        out = (y * cos + rot * sin) * post
        o_ref[:, cols(c)] = jnp.where(is_v, acc, out).astype(o_ref.dtype)

    _pipelined(o_ref.shape[1] // cw, project, finish)


def matmul_qkv(x, w, n_q_cols, n_k_cols, g_q, g_k, q_scale, rope_tables=None, ssq=None, *, tm=1024, tn=1024):
    m, kd = x.shape
    n = w.shape[1]
    hd = g_q.shape[0]
    assert hd == LANES
    n_v_cols = n - n_q_cols - n_k_cols
    tn = _tile(math.gcd(n_q_cols, n_k_cols, n_v_cols), tn, LANES)
    tm, tn, tk, nk = _mm_tiles(m, n, kd, tm, tn)
    in_specs = [pl.BlockSpec((tm, tk), lambda i, j, k: (i, k)),
                pl.BlockSpec((tk, tn), lambda i, j, k: (k, j)),
                pl.BlockSpec((1, hd), lambda i, j, k: (0, 0)),
                pl.BlockSpec((1, hd), lambda i, j, k: (0, 0))]
    args = [x, w, g_q.reshape(1, hd), g_k.reshape(1, hd)]
    blocks = dict(n_q_blocks=n_q_cols // tn, n_qk_blocks=(n_q_cols + n_k_cols) // tn, q_scale=q_scale,
                  norm_dim=kd if ssq is not None else 0)
    if rope_tables is None:
        body = functools.partial(_mm_qkv_body, nk=nk, **blocks)
    else:
        assert nk == 1 and tn % (2 * hd) == 0
        cos, sin, perm = rope_tables
        in_specs += [pl.BlockSpec((tm, hd), lambda i, j, k: (i, 0))] * 2
        in_specs.append(pl.BlockSpec((2 * hd, 2 * hd), lambda i, j, k: (0, 0)))
        args += [cos, sin, perm]
        body = functools.partial(_mm_qkv_rope_body, **blocks)
    if ssq is not None:
        in_specs.append(_ssq_spec(ssq, tm))
        args.append(ssq)
    return pl.pallas_call(
        body,
        out_shape=jax.ShapeDtypeStruct((m, n), BF16),
        grid=(m // tm, n // tn, nk),
        in_specs=in_specs,
        out_specs=pl.BlockSpec((tm, tn), lambda i, j, k: (i, j)),
        scratch_shapes=_acc_scratch(nk, 1, tm, tn),
        compiler_params=_params("parallel", "parallel", "arbitrary"),
        name="matmul_qkv",
    )(*args)


def _row_rms_factor(ssq_ref, d):
    return lax.rsqrt(ssq_ref[...][:, :1] * (1.0 / d) + EPS)


def _ssq_spec(ssq, tm):
    return pl.BlockSpec((tm, ssq.shape[1]), lambda i, j, k: (i, 0))


def _mm_resid_body(*refs, nk, has_bias, r_parts, o_parts, norm_stats):
    n_r, n_o = len(r_parts.tiles), len(o_parts.tiles)
    x_ref, w_ref = refs[:2]
    r_refs = refs[2:2 + n_r]
    b_ref = refs[2 + n_r] if has_bias else None
    o_refs = refs[2 + n_r + has_bias:2 + n_r + has_bias + n_o]
    rest = refs[2 + n_r + has_bias + n_o:]
    (xb_ref, ssq_ref), acc = (rest[:2], rest[2:]) if norm_stats else ((None, None), rest)
    i = pl.program_id(0)

    if norm_stats:
        @pl.when(pl.program_id(1) == 0)
        def _():
            ssq_ref[...] = jnp.zeros(ssq_ref.shape, F32)

    def finish(s):
        out = s[0] + b_ref[...] if has_bias else s[0]
        out = r_parts.select(i, r_refs) + out
        if norm_stats:
            xb_ref[...] = out.astype(xb_ref.dtype)
            ssq_ref[...] += jnp.broadcast_to(jnp.sum(out * out, axis=-1, keepdims=True), ssq_ref.shape)
        if n_o == 1:
            o_refs[0][...] = out
        else:
            for p, o_ref in enumerate(o_refs):
                @pl.when(o_parts.active(p, i))
                def _(o_ref=o_ref):
                    o_ref[...] = out

    _matmul_steps(nk, [jnp.dot(x_ref[...], w_ref[...], preferred_element_type=F32)], acc, finish)


def matmul_residual(x, w, resid, bias=None, out_rows=None, norm_stats=False, *, tm=1024, tn=512):
    m, kd = x.shape
    n = w.shape[1]
    if kd > 4096:
        tm = tm // 2
    r_rows = [r.shape[0] for r in resid]
    o_rows = list(out_rows) if out_rows else [m]
    tm = min(tm, math.gcd(*r_rows, *o_rows))
    tm, tn, tk, nk = _mm_tiles(m, n, kd, tm, tn, tk=8192)
    nj = n // tn
    r_parts, o_parts = _RowParts(r_rows, tm), _RowParts(o_rows, tm)

    def part_spec(parts, p):
        return pl.BlockSpec((tm, tn), lambda i, j, k: (parts.row_tile(p, i), parts.col_tile(p, i, j, nj)))

    in_specs = [pl.BlockSpec((tm, tk), lambda i, j, k: (i, k)),
                pl.BlockSpec((tk, tn), lambda i, j, k: (k, j))]
    in_specs += [part_spec(r_parts, p) for p in range(len(resid))]
    args = [x, w, *resid]
    if bias is not None:
        in_specs.append(pl.BlockSpec((1, tn), lambda i, j, k: (0, j)))
        args.append(bias.reshape(1, n))
    out_shape = [jax.ShapeDtypeStruct((r, n), F32) for r in o_rows]
    out_specs = [part_spec(o_parts, p) for p in range(len(o_rows))]
    if norm_stats:
        assert out_rows is None and nk == 1
        out_shape += [jax.ShapeDtypeStruct((m, n), BF16), jax.ShapeDtypeStruct((m, LANES), F32)]
        out_specs += [pl.BlockSpec((tm, tn), lambda i, j, k: (i, j)),
                      pl.BlockSpec((tm, LANES), lambda i, j, k: (i, 0))]
    out = pl.pallas_call(
        functools.partial(_mm_resid_body, nk=nk, has_bias=bias is not None, r_parts=r_parts, o_parts=o_parts,
                          norm_stats=norm_stats),
        out_shape=out_shape,
        grid=(m // tm, nj, nk),
        in_specs=in_specs,
        out_specs=out_specs,
        scratch_shapes=_acc_scratch(nk, 1, tm, tn),
        compiler_params=(_params("arbitrary", "arbitrary", "arbitrary") if out_rows
                         else _params("parallel", "arbitrary" if norm_stats else "parallel", "arbitrary")),
        name="matmul_residual",
    )(*args)
    return out if (out_rows or norm_stats) else out[0]


def _mm_glu_body(x_ref, wa_ref, wg_ref, ba_ref, bg_ref, *rest, nk, norm_dim):
    ssq_ref, (o_ref, *acc) = _split_norm_refs(rest, norm_dim)

    def finish(s):
        if norm_dim:
            rms = _row_rms_factor(ssq_ref, norm_dim)
            s = [part * rms for part in s]
        a = s[0] + ba_ref[...]
        g = s[1] + bg_ref[...]
        o_ref[...] = (a * jax.nn.sigmoid(g)).astype(o_ref.dtype)

    x = x_ref[...]
    _matmul_steps(nk, [jnp.dot(x, wa_ref[...], preferred_element_type=F32),
                       jnp.dot(x, wg_ref[...], preferred_element_type=F32)], acc, finish)


def matmul_glu(x, w, b, ssq=None, *, tm=1024, tn=512):
    m, kd = x.shape
    n = w.shape[1] // 2
    tm, tn, tk, nk = _mm_tiles(m, n, kd, tm, tn)
    nj = n // tn
    b2 = b.reshape(1, 2 * n)
    return pl.pallas_call(
        functools.partial(_mm_glu_body, nk=nk, norm_dim=kd if ssq is not None else 0),
        out_shape=jax.ShapeDtypeStruct((m, n), BF16),
        grid=(m // tm, nj, nk),
        in_specs=[pl.BlockSpec((tm, tk), lambda i, j, k: (i, k)),
                  pl.BlockSpec((tk, tn), lambda i, j, k: (k, j)),
                  pl.BlockSpec((tk, tn), lambda i, j, k: (k, j + nj)),
                  pl.BlockSpec((1, tn), lambda i, j, k: (0, j)),
                  pl.BlockSpec((1, tn), lambda i, j, k: (0, j + nj))]
        + ([_ssq_spec(ssq, tm)] if ssq is not None else []),
        out_specs=pl.BlockSpec((tm, tn), lambda i, j, k: (i, j)),
        scratch_shapes=_acc_scratch(nk, 2, tm, tn),
        compiler_params=_params("parallel", "parallel", "arbitrary"),
        name="matmul_glu",
    )(x, w, w, b2, b2, *([ssq] if ssq is not None else []))


def _mm_swiglu_body(*refs, nk, gated, norm_dim=0):
    x_ref, w1_ref, w3_ref = refs[:3]
    g_ref = refs[3] if gated else None
    ssq_ref, (o_ref, *acc) = _split_norm_refs(refs[3 + gated:], norm_dim)

    def finish(s):
        if norm_dim:
            rms = _row_rms_factor(ssq_ref, norm_dim)
            s = [part * rms for part in s]
        u = jax.nn.silu(s[0]) * s[1]
        if gated:
            gates = g_ref[...]
            lane = lax.broadcasted_iota(jnp.int32, gates.shape, 1)
            gate = jnp.sum(jnp.where(lane == pl.program_id(1), gates, 0.0), axis=-1, keepdims=True)
            u = u * gate
        o_ref[...] = u.astype(o_ref.dtype)

    x = x_ref[...]
    _matmul_steps(nk, [jnp.dot(x, w1_ref[...], preferred_element_type=F32),
                       jnp.dot(x, w3_ref[...], preferred_element_type=F32)], acc, finish)


def matmul_swiglu(x, w1, w3, ssq=None, *, tm=1024, tn=512):
    m, kd = x.shape
    n = w1.shape[1]
    tm, tn, tk, nk = _mm_tiles(m, n, kd, tm, tn)
    return pl.pallas_call(
        functools.partial(_mm_swiglu_body, nk=nk, gated=False, norm_dim=kd if ssq is not None else 0),
        out_shape=jax.ShapeDtypeStruct((m, n), BF16),
        grid=(m // tm, n // tn, nk),
        in_specs=[pl.BlockSpec((tm, tk), lambda i, j, k: (i, k)),
                  pl.BlockSpec((tk, tn), lambda i, j, k: (k, j)),
                  pl.BlockSpec((tk, tn), lambda i, j, k: (k, j))]
        + ([_ssq_spec(ssq, tm)] if ssq is not None else []),
        out_specs=pl.BlockSpec((tm, tn), lambda i, j, k: (i, j)),
        scratch_shapes=_acc_scratch(nk, 2, tm, tn),
        compiler_params=_params("parallel", "parallel", "arbitrary"),
        name="matmul_swiglu",
    )(x, w1, w3, *([ssq] if ssq is not None else []))


def matmul_swiglu_experts(x, w1, w3, gates, *, tm=1024):
    m, kd = x.shape
    e, _, f = w1.shape
    tm, _, tk, nk = _mm_tiles(m, f, kd, tm, f)
    return pl.pallas_call(
        functools.partial(_mm_swiglu_body, nk=nk, gated=True),
        out_shape=jax.ShapeDtypeStruct((m, e * f), BF16),
        grid=(m // tm, e, nk),
        in_specs=[pl.BlockSpec((tm, tk), lambda i, j, k: (i, k)),
                  pl.BlockSpec((None, tk, f), lambda i, j, k: (j, k, 0)),
                  pl.BlockSpec((None, tk, f), lambda i, j, k: (j, k, 0)),
                  pl.BlockSpec((tm, LANES), lambda i, j, k: (i, 0))],
        out_specs=pl.BlockSpec((tm, f), lambda i, j, k: (i, j)),
        scratch_shapes=_acc_scratch(nk, 2, tm, f),
        compiler_params=_params("parallel", "parallel", "arbitrary"),
        name="matmul_swiglu_experts",
    )(x, w1, w3, gates)


def _rope_tables(seq_lens):
    half = LANES // 2
    inv_freq = ROPE_THETA ** (-jnp.arange(0, half, 2, dtype=F32) / half)
    cos_l, sin_l = [], []
    for s in seq_lens:
        t = jnp.arange(s, dtype=jnp.int32)
        row = (t // GRID_W).astype(F32)
        col = (t % GRID_W).astype(F32)
        ang_r = row[:, None] * inv_freq[None, :]
        ang_c = col[:, None] * inv_freq[None, :]
        emb = jnp.concatenate([ang_r, ang_r, ang_c, ang_c], axis=-1)
        cos_l.append(jnp.cos(emb))
        sin_l.append(jnp.sin(emb))
    cos = jnp.concatenate(cos_l, axis=0)
    sin = jnp.concatenate(sin_l, axis=0)
    quarter = half // 2
    perm = np.zeros((2 * LANES, 2 * LANES), np.float32)
    for i in range(2 * LANES):
        if i % half < quarter:
            perm[i + quarter, i] = -1.0
        else:
            perm[i - quarter, i] = 1.0
    return cos, sin, jnp.asarray(perm, BF16)


def _transpose_body(x_ref, o_ref):
    o_ref[...] = x_ref[...].astype(F32).T.astype(o_ref.dtype)


def transpose_cols(x, col0, ncols, *, tm=512, bw=512):
    m = x.shape[0]
    tm = _tile(m, tm, LANES)
    bw = _tile(math.gcd(ncols, col0) if col0 else ncols, bw, LANES)
    c0 = col0 // bw
    return pl.pallas_call(
        _transpose_body,
        out_shape=jax.ShapeDtypeStruct((ncols, m), x.dtype),
        grid=(m // tm, ncols // bw),
        in_specs=[pl.BlockSpec((tm, bw), lambda i, j: (i, c0 + j))],
        out_specs=pl.BlockSpec((bw, tm), lambda i, j: (j, i)),
        compiler_params=_params("parallel", "parallel"),
        name="transpose_cols",
    )(x)


def _step_tables(seq_lens, tq, tk):
    qt, kt, first, last, qrel, krel, side = [], [], [], [], [], [], []
    start = 0
    for s in seq_lens:
        for qi in range(s // tq):
            nkv = s // tk
            for ki in range(nkv):
                qt.append(start // tq + qi)
                kt.append(start // tk + ki)
                first.append(int(ki == 0))
                last.append(int(ki == nkv - 1))
                qrel.append(qi * tq)
                krel.append(ki * tk)
                side.append(1 if (ki + 1) * tk <= qi * tq else (-1 if ki * tk >= (qi + 1) * tq else 0))
        start += s
    return [jnp.asarray(np.asarray(a, np.int32)) for a in (qt, kt, first, last, qrel, krel, side)]


ONES_ROWS = 16


def _with_ones_rows(vt):
    return jnp.concatenate([vt, jnp.ones((ONES_ROWS, vt.shape[1]), vt.dtype)], axis=0)


def _online_softmax_step(st, vt1, m_ref, acc_ref, idx, shift=None):
    m_prev = m_ref[idx]
    tile_max = jnp.max(st, axis=0, keepdims=True)
    if shift is not None:
        tile_max = tile_max + shift
    m_new = jnp.maximum(m_prev, tile_max)
    alpha = jnp.exp2(m_prev - m_new)
    p = jnp.exp2(st - (m_new if shift is None else m_new - shift))
    acc_ref[idx] = alpha * acc_ref[idx] + jnp.dot(vt1, p.astype(vt1.dtype), preferred_element_type=F32)
    m_ref[idx] = m_new


SAFE_LOG2 = 60.0


def _score_bound(g_q, g_k, q_scale):
    bound = q_scale * LANES * jnp.max(jnp.abs(g_q)) * jnp.max(jnp.abs(g_k))
    return (bound <= SAFE_LOG2).astype(jnp.int32).reshape(1)


def _pv_fixed_reference(st, vt1):
    return jnp.dot(vt1, jnp.exp2(st).astype(vt1.dtype), preferred_element_type=F32)


def _scores_t(k, q):
    return lax.dot_general(k, q, (((1,), (1,)), ((), ())), preferred_element_type=F32)


def _init_softmax_state(m_ref, acc_ref):
    m_ref[...] = jnp.full(m_ref.shape, -jnp.inf, F32)
    acc_ref[...] = jnp.zeros(acc_ref.shape, F32)


def _gqa_body(qt_ref, kt_ref, first_ref, last_ref, qrel_ref, krel_ref, side_ref, bounded_ref,
              q_ref, k_ref, vt_ref, o_ref, m_ref, acc_ref, *, group, kc):
    s = pl.program_id(1)

    @pl.when(first_ref[s] == 1)
    def _():
        _init_softmax_state(m_ref, acc_ref)

    def scores(c):
        k = k_ref[c * kc:(c + 1) * kc, :]
        return [_scores_t(k, q_ref[:, g * LANES:(g + 1) * LANES]) for g in range(group)]

    @pl.when(bounded_ref[0] == 1)
    def _():
        pv = [[] for _ in range(group)]

        def softmax(c, sts):
            vt1 = _with_ones_rows(vt_ref[:, c * kc:(c + 1) * kc])
            for g in range(group):
                pv[g].append(_pv_fixed_reference(sts[g], vt1))

        _pipelined(k_ref.shape[0] // kc, scores, softmax)
        for g in range(group):
            acc_ref[g] += functools.reduce(lambda a, b: a + b, pv[g])

    @pl.when(bounded_ref[0] == 0)
    def _():
        def softmax(c, sts):
            vt1 = _with_ones_rows(vt_ref[:, c * kc:(c + 1) * kc])
            for g in range(group):
                _online_softmax_step(sts[g], vt1, m_ref, acc_ref, g)

        _pipelined(k_ref.shape[0] // kc, scores, softmax)

    @pl.when(last_ref[s] == 1)
    def _():
        for g in range(group):
            acc = acc_ref[g]
            o = acc[:LANES] / acc[LANES:LANES + 1]
            o_ref[:, g * LANES:(g + 1) * LANES] = o.T.astype(o_ref.dtype)


def gqa_attention(qk, vt, bounded, seq_lens, n_heads, n_kv, *, tq=1024, tk=2048, kc=512):
    m = qk.shape[0]
    group = n_heads // n_kv
    g_all = math.gcd(*seq_lens)
    tq, tk = _tile(g_all, tq, LANES), _tile(g_all, tk, LANES)
    tabs = _step_tables(seq_lens, tq, tk)
    n_steps = tabs[0].shape[0]
    return pl.pallas_call(
        functools.partial(_gqa_body, group=group, kc=_tile(tk, kc, LANES)),
        out_shape=jax.ShapeDtypeStruct((m, n_heads * LANES), BF16),
        grid_spec=pltpu.PrefetchScalarGridSpec(
            num_scalar_prefetch=8,
            grid=(n_kv, n_steps),
            in_specs=[pl.BlockSpec((tq, group * LANES), lambda h, s, qt, kt, *_: (qt[s], h)),
                      pl.BlockSpec((tk, LANES), lambda h, s, qt, kt, *_: (kt[s], n_heads + h)),
                      pl.BlockSpec((LANES, tk), lambda h, s, qt, kt, *_: (h, kt[s]))],
            out_specs=pl.BlockSpec((tq, group * LANES), lambda h, s, qt, kt, *_: (qt[s], h)),
            scratch_shapes=[pltpu.VMEM((group, 1, tq), F32),
                            pltpu.VMEM((group, LANES + ONES_ROWS, tq), F32)]),
        compiler_params=_params("parallel", "arbitrary"),
        name="gqa_attention",
    )(*tabs, bounded, qk, qk, vt)


ALIBI_SPLIT = 3


def _diff_body(qt_ref, kt_ref, first_ref, last_ref, qrel_ref, krel_ref, side_ref, bounded_ref, sgn_ref, slope_ref,
               q_ref, k_ref, vt_ref, kpos_ref, ext_ref, lq1_ref, lk1_ref, lq2_ref, lk2_ref, gsub_ref,
               o_ref, m_ref, acc_ref, *, lambda_init, kc):
    h = pl.program_id(0)
    s = pl.program_id(1)

    @pl.when(first_ref[s] == 1)
    def _():
        _init_softmax_state(m_ref, acc_ref)

    tq, tk = q_ref.shape[0], k_ref.shape[0]
    dv = vt_ref.shape[0]
    slope2 = slope_ref[h]
    subs = [slice(c * LANES, (c + 1) * LANES) for c in range(2)]
    rows = lambda c: slice(c * kc, (c + 1) * kc)

    def online(scores, shift=None):
        def softmax(c, sts):
            vt1 = _with_ones_rows(vt_ref[:, rows(c)])
            for sub in range(2):
                _online_softmax_step(sts[sub], vt1, m_ref, acc_ref, sub, shift)

        _pipelined(tk // kc, scores, softmax)

    def fixed_reference(scores):
        pv = [[], []]

        def softmax(c, sts):
            vt1 = _with_ones_rows(vt_ref[:, rows(c)])
            for sub in range(2):
                pv[sub].append(_pv_fixed_reference(sts[sub], vt1))

        _pipelined(tk // kc, scores, softmax)
        for sub in range(2):
            acc_ref[sub] += functools.reduce(lambda a, b: a + b, pv[sub])

    bounded = bounded_ref[0] == 1

    @pl.when(side_ref[s] == 0)
    def _():
        keys = lax.broadcasted_iota(jnp.int32, (kc, tq), 0)
        queries = lax.broadcasted_iota(jnp.int32, (kc, tq), 1)
        delta = keys - queries + (krel_ref[s] - qrel_ref[s])

        def scores(c):
            bias = jnp.abs(delta + c * kc).astype(F32) * (-slope2)
            return [_scores_t(k_ref[rows(c), sl], q_ref[:, sl]) + bias for sl in subs]

        pl.when(bounded)(lambda: fixed_reference(scores))
        pl.when(jnp.logical_not(bounded))(lambda: online(scores))

    @pl.when(side_ref[s] != 0)
    def _():
        sgn = sgn_ref[s]
        ext_row = ext_ref[...] * sgn

        def augmented(q_ext):
            q_aug = [jnp.concatenate([q_ref[:, sl], q_ext.astype(q_ref.dtype)], axis=1) for sl in subs]

            def scores(c):
                k_ext = kpos_ref[rows(c), :]
                return [_scores_t(jnp.concatenate([k_ref[rows(c), sl], k_ext], axis=1), q_aug[sub])
                        for sub, sl in enumerate(subs)]

            return scores

        @pl.when(bounded)
        def _():
            qpos = qrel_ref[s] + lax.broadcasted_iota(jnp.int32, (tq, LANES), 0)
            rest = (sgn * slope2) * (krel_ref[s] - qpos).astype(F32)
            lane = lax.broadcasted_iota(jnp.int32, (tq, LANES), 1)
            q_ext = jnp.broadcast_to(ext_row, (tq, LANES))
            for i in range(ALIBI_SPLIT):
                piece = rest.astype(BF16).astype(F32)
                q_ext = jnp.where(lane == 2 * ALIBI_SPLIT + i, piece, q_ext)
                rest = rest - piece
            fixed_reference(augmented(q_ext))

        @pl.when(jnp.logical_not(bounded))
        def _():
            qpos = qrel_ref[s] + lax.broadcasted_iota(jnp.int32, (1, tq), 1)
            shift = (sgn * slope2) * (krel_ref[s] - qpos).astype(F32)
            online(augmented(jnp.broadcast_to(ext_row, (tq, LANES))), shift)

    @pl.when(last_ref[s] == 1)
    def _():
        lam = (jnp.exp(jnp.sum(lq1_ref[...] * lk1_ref[...], axis=-1, keepdims=True))
               - jnp.exp(jnp.sum(lq2_ref[...] * lk2_ref[...], axis=-1, keepdims=True)) + lambda_init)
        acc0, acc1 = acc_ref[0], acc_ref[1]
        o = (acc0[:dv] * (1.0 / acc0[dv:dv + 1])
             - acc1[:dv] * (lam * (1.0 / acc1[dv:dv + 1])))
        r = lax.rsqrt(jnp.mean(o * o, axis=0, keepdims=True) + EPS)
        o_ref[...] = (((o * r).T * gsub_ref[...]) * (1.0 - lambda_init)).astype(o_ref.dtype)


def _alibi_tables(n_heads, tk):
    slope2 = (2.0 ** (-8.0 * np.arange(1, n_heads + 1, dtype=np.float64) / n_heads) * LOG2E).astype(np.float32)
    to_bf16 = lambda a: a.astype(BF16).astype(np.float32)
    pieces, rest = [], slope2.copy()
    for _ in range(ALIBI_SPLIT):
        c = to_bf16(rest)
        pieces.append(c)
        rest = (rest - c).astype(np.float32)
    ext = np.zeros((n_heads, 1, LANES), np.float32)
    kpos = np.zeros((tk, LANES), np.float32)
    j = np.arange(tk)
    for i, c in enumerate(pieces):
        ext[:, 0, i] = float(LANES) * c
        ext[:, 0, ALIBI_SPLIT + i] = c
        kpos[:, i] = j // LANES
        kpos[:, ALIBI_SPLIT + i] = j % LANES
        kpos[:, 2 * ALIBI_SPLIT + i] = 1.0
    assert tk // LANES <= 256, "key offsets must stay exact in bf16"
    return jnp.asarray(slope2), jnp.asarray(ext), jnp.asarray(kpos, BF16)


def diff_attention(qk, vt, bounded, seq_lens, n_heads, lq1, lk1, lq2, lk2, g_sub, lambda_init,
                   *, tq=1024, tk=2048, kc=512):
    m = qk.shape[0]
    dv = 2 * LANES
    g_all = math.gcd(*seq_lens)
    tq, tk = _tile(g_all, tq, LANES), _tile(g_all, tk, LANES)
    tabs = _step_tables(seq_lens, tq, tk)
    n_steps = tabs[0].shape[0]
    sgn = tabs[-1].astype(F32)
    slope2, ext, kpos = _alibi_tables(n_heads, tk)
    vec = lambda a: a.reshape(1, -1)
    small = pl.BlockSpec((1, LANES), lambda h, s, *_: (0, 0))
    return pl.pallas_call(
        functools.partial(_diff_body, lambda_init=lambda_init, kc=_tile(tk, kc, LANES)),
        out_shape=jax.ShapeDtypeStruct((m, n_heads * dv), BF16),
        grid_spec=pltpu.PrefetchScalarGridSpec(
            num_scalar_prefetch=10,
            grid=(n_heads, n_steps),
            in_specs=[pl.BlockSpec((tq, dv), lambda h, s, qt, kt, *_: (qt[s], h)),
                      pl.BlockSpec((tk, dv), lambda h, s, qt, kt, *_: (kt[s], n_heads + h)),
                      pl.BlockSpec((dv, tk), lambda h, s, qt, kt, *_: (h, kt[s])),
                      pl.BlockSpec((tk, LANES), lambda h, s, *_: (0, 0)),
                      pl.BlockSpec((None, 1, LANES), lambda h, s, *_: (h, 0, 0)),
                      small, small, small, small,
                      pl.BlockSpec((1, dv), lambda h, s, *_: (0, 0))],
            out_specs=pl.BlockSpec((tq, dv), lambda h, s, qt, kt, *_: (qt[s], h)),
            scratch_shapes=[pltpu.VMEM((2, 1, tq), F32),
                            pltpu.VMEM((2, dv + ONES_ROWS, tq), F32)]),
        compiler_params=_params("parallel", "arbitrary"),
        name="diff_attention",
    )(*tabs, bounded, sgn, slope2, qk, qk, vt, kpos, ext, vec(lq1), vec(lk1), vec(lq2), vec(lk2), vec(g_sub))


HALO = 16


def _dwconv_ln_silu_body(hp_ref, hn_ref, prev_ref, cur_ref, next_ref, w_ref, b_ref, g_ref, beta_ref,
                         o_ref, win_ref, conv_ref, *, width, lc):
    i = pl.program_id(0)
    tt, d = cur_ref.shape
    pad = width // 2
    zeros = jnp.zeros(prev_ref.shape, F32)
    win_ref[0:HALO, :] = jnp.where(hp_ref[i] == 1, prev_ref[...].astype(F32), zeros)
    win_ref[HALO:HALO + tt, :] = cur_ref[...].astype(F32)
    win_ref[HALO + tt:, :] = jnp.where(hn_ref[i] == 1, next_ref[...].astype(F32), zeros)
    rc = 64
    first = HALO - pad
    span = rc + SUBLANES * ((width - 1) // SUBLANES) + SUBLANES

    def lane_chunk(c, carry):
        lanes = pl.ds(pl.multiple_of(c * lc, lc), lc)
        for r0 in range(0, tt, rc):
            acc = jnp.zeros((rc, lc), F32) + b_ref[:, lanes]
            block = win_ref[r0:r0 + span, lanes]
            for phase in range(SUBLANES):
                shift = (first + phase) % SUBLANES
                rolled = block if shift == 0 else pltpu.roll(block, span - shift, 0)
                for t in range(phase, width, SUBLANES):
                    a = first + t - shift
                    acc = acc + rolled[a:a + rc, :] * w_ref[t:t + 1, lanes]
            conv_ref[r0:r0 + rc, lanes] = acc
        return carry

    lax.fori_loop(0, d // lc, lane_chunk, 0)

    x = conv_ref[...]
    mu = jnp.mean(x, axis=-1, keepdims=True)
    xc = x - mu
    var = jnp.mean(xc * xc, axis=-1, keepdims=True)
    y = xc * lax.rsqrt(var + EPS) * g_ref[...] + beta_ref[...]
    o_ref[...] = jax.nn.silu(y).astype(o_ref.dtype)


def dwconv_ln_silu(u, w_dw, b_dw, ln_g, ln_b, seq_lens, *, tt=256, lc=512):
    m, d = u.shape
    width = w_dw.shape[0]
    assert width // 2 <= HALO
    tt = _tile(math.gcd(*seq_lens), tt, 64)
    lc = _tile(d, lc, LANES)
    nb = tt // HALO
    starts = np.cumsum([0] + list(seq_lens))
    has_prev = np.ones(m // tt, np.int32)
    has_next = np.ones(m // tt, np.int32)
    for st in starts[:-1]:
        has_prev[st // tt] = 0
    for en in starts[1:]:
        has_next[en // tt - 1] = 0
    last_halo = m // HALO - 1
    row = lambda a: a.reshape(1, d)
    vec_spec = pl.BlockSpec((1, d), lambda i, *_: (0, 0))
    return pl.pallas_call(
        functools.partial(_dwconv_ln_silu_body, width=width, lc=lc),
        out_shape=jax.ShapeDtypeStruct((m, d), BF16),
        grid_spec=pltpu.PrefetchScalarGridSpec(
            num_scalar_prefetch=2,
            grid=(m // tt,),
            in_specs=[pl.BlockSpec((HALO, d), lambda i, *_: (jnp.maximum(i * nb - 1, 0), 0)),
                      pl.BlockSpec((tt, d), lambda i, *_: (i, 0)),
                      pl.BlockSpec((HALO, d), lambda i, *_: (jnp.minimum((i + 1) * nb, last_halo), 0)),
                      pl.BlockSpec((width, d), lambda i, *_: (0, 0)),
                      vec_spec, vec_spec, vec_spec],
            out_specs=pl.BlockSpec((tt, d), lambda i, *_: (i, 0)),
            scratch_shapes=[pltpu.VMEM((tt + 2 * HALO, d), F32), pltpu.VMEM((tt, d), F32)]),
        compiler_params=_params("parallel"),
        name="dwconv_ln_silu",
    )(jnp.asarray(has_prev), jnp.asarray(has_next), u, u, u, w_dw, row(b_dw), row(ln_g), row(ln_b))


def _cast_body(*refs, scaled):
    if scaled:
        w_ref, g_ref, o_ref = refs
        o_ref[...] = (w_ref[...] * g_ref[...]).astype(o_ref.dtype)
    else:
        w_ref, o_ref = refs
        o_ref[...] = w_ref[...].astype(o_ref.dtype)


def cast_layer(w, idx, row_scale=None, *, block_bytes=4 * 1024 * 1024):
    lead, tail = w.shape[0], w.shape[1:]
    cols = tail[-1]
    rows = math.prod(tail[:-1])
    tc = _tile(cols, 2048, LANES)
    tr = _tile(rows, max(16, block_bytes // (4 * tc)), 16)
    in_specs = [pl.BlockSpec((None, tr, tc), lambda i, j: (idx, i, j))]
    args = [w.reshape(lead, rows, cols)]
    if row_scale is not None:
        in_specs.append(pl.BlockSpec((tr, 1), lambda i, j: (i, 0)))
        args.append(row_scale.reshape(rows, 1))
    out = pl.pallas_call(
        functools.partial(_cast_body, scaled=row_scale is not None),
        out_shape=jax.ShapeDtypeStruct((rows, cols), BF16),
        grid=(rows // tr, cols // tc),
        in_specs=in_specs,
        out_specs=pl.BlockSpec((tr, tc), lambda i, j: (i, j)),
        compiler_params=_params("parallel", "parallel"),
        name="cast_layer",
    )(*args)
    return out.reshape(tail)


def kernel(x_prompt, x_sample, norm_mix, norm_ffn, a_w_qkv, a_w_o, a_q_norm, a_k_norm, b_w_pw1, b_b_pw1, b_w_dw, b_b_dw, b_ln_g, b_ln_b, b_w_pw2, b_b_pw2, c_w_qkv, c_w_o, c_q_norm, c_k_norm, c_lambda_q1, c_lambda_k1, c_lambda_q2, c_lambda_k2, c_subln, ffn_w1, ffn_w3, ffn_w2, moe_router, moe_w1, moe_w3, moe_w2):
    depth, d = norm_mix.shape
    bp, sp, _ = x_prompt.shape
    bs, ss, _ = x_sample.shape
    seq_lens = [sp] * bp + [ss] * bs
    mp, ms = bp * sp, bs * ss
    x = [x_prompt.reshape(mp, d), x_sample.reshape(ms, d)]

    hd = a_q_norm.shape[-1]
    a_heads = d // hd
    a_kv = (a_w_qkv.shape[-1] - d) // (2 * hd)
    c_heads = d // (2 * c_q_norm.shape[-1])
    n_exp, _, f_exp = moe_w1.shape[1:]
    rope_tables = _rope_tables(seq_lens)
    bf = cast_layer

    def residual(u, w, x, bias=None, want_stats=False, out_rows=None):
        if want_stats:
            x_new, xb, ssq = matmul_residual(u, w, x, bias, norm_stats=True)
            return [x_new], (xb, ssq)
        out = matmul_residual(u, w, x, bias, out_rows=out_rows)
        return (out if out_rows else [out]), None

    def normed_input(x, stats, gain):
        if stats is None:
            return rmsnorm(x, gain), None, None
        return stats[0], stats[1], gain

    stats = None
    for i in range(depth):
        h, ssq, gain = normed_input(x, stats, norm_mix[i])
        mixer, j = i % N_MIXERS, i // N_MIXERS
        dense_ffn = i % 2 == 0
        if mixer == 0:
            q_scale = hd ** -0.5 * LOG2E
            qkv = matmul_qkv(h, bf(a_w_qkv, j, gain), d, a_kv * hd, a_q_norm[j], a_k_norm[j], q_scale,
                             rope_tables, ssq)
            vt = transpose_cols(qkv, d + a_kv * hd, a_kv * hd)
            bounded = _score_bound(a_q_norm[j], a_k_norm[j], q_scale)
            o = gqa_attention(qkv, vt, bounded, seq_lens, a_heads, a_kv)
            x, stats = residual(o, bf(a_w_o, j), x, want_stats=dense_ffn)
        elif mixer == 1:
            u = matmul_glu(h, bf(b_w_pw1, j, gain), b_b_pw1[j], ssq)
            u = dwconv_ln_silu(u, b_w_dw[j], b_b_dw[j], b_ln_g[j], b_ln_b[j], seq_lens)
            x, stats = residual(u, bf(b_w_pw2, j), x, b_b_pw2[j], want_stats=dense_ffn)
        else:
            lambda_init = 0.8 - 0.6 * math.exp(-0.3 * i)
            q_scale = c_q_norm.shape[-1] ** -0.5 * LOG2E
            qkv = matmul_qkv(h, bf(c_w_qkv, j, gain), d, d, c_q_norm[j], c_k_norm[j], q_scale, None, ssq)
            vt = transpose_cols(qkv, 2 * d, d)
            bounded = _score_bound(c_q_norm[j], c_k_norm[j], q_scale)
            o = diff_attention(qkv, vt, bounded, seq_lens, c_heads, c_lambda_q1[j], c_lambda_k1[j],
                               c_lambda_q2[j], c_lambda_k2[j], c_subln[j], lambda_init)
            x, stats = residual(o, bf(c_w_o, j), x, want_stats=dense_ffn)
        k = i // 2
        last = i == depth - 1
        out_rows = [mp, ms] if last else None
        if dense_ffn:
            h, ssq, gain = normed_input(x, stats, norm_ffn[i])
            u = matmul_swiglu(h, bf(ffn_w1, k, gain), bf(ffn_w3, k, gain), ssq)
            x, stats = residual(u, bf(ffn_w2, k), x, want_stats=not last, out_rows=out_rows)
        else:
            h, gates = rmsnorm_router(x[0], norm_ffn[i], moe_router[k])
            u = matmul_swiglu_experts(h, bf(moe_w1, k), bf(moe_w3, k), gates)
            x, stats = residual(u, bf(moe_w2, k).reshape(n_exp * f_exp, d), x, want_stats=not last,
                                out_rows=out_rows)

    return (x[0].reshape(bp, sp, d), x[1].reshape(bs, ss, d))
```
